```python
import functools
import jax, jax.numpy as jnp
from jax import lax
import numpy as np

D_MODEL = 1024
BATCH = 4
SEQ = 8192
DEPTH = 1
DEC_BATCH = 128
DEC_SEQ = 1
PAST_LEN = 16384
PAGE_SIZE = 128

N_HEADS = 8
QK_NOPE = 64
QK_ROPE = 32
V_DIM = 64
Q_RANK = 384
KV_RANK = 256
ATTN_WIDTH = N_HEADS * V_DIM
CONV_WIDTH = 512
CONV_K = 31
D_FF = 2816
FFN_K = 3
D_IN = Q_RANK + KV_RANK + QK_ROPE + 2 * CONV_WIDTH
D_MIX = ATTN_WIDTH + CONV_WIDTH
N_MOD = 6
ROPE_THETA = 10000.0
EPS = 1e-6
Q_BLOCK = 128
ATTN_SCALE = (QK_NOPE + QK_ROPE) ** -0.5

kernel_name = 'hybrid_mla_conformer_convffn_adaln_step'


def rmsnorm(x, g):
    xf = x.astype(jnp.float32)
    y = xf * lax.rsqrt(jnp.mean(xf * xf, axis=-1, keepdims=True) + EPS)
    return (y * g.astype(jnp.float32)).astype(x.dtype)


def layernorm(x, g, b):
    xf = x.astype(jnp.float32)
    mu = jnp.mean(xf, axis=-1, keepdims=True)
    d = xf - mu
    y = d * lax.rsqrt(jnp.mean(d * d, axis=-1, keepdims=True) + EPS)
    return (y * g.astype(jnp.float32) + b.astype(jnp.float32)).astype(x.dtype)


def rope(x, pos):
    half = QK_ROPE // 2
    freqs = ROPE_THETA ** (-jnp.arange(half, dtype=jnp.float32) / half)
    ang = pos.astype(jnp.float32)[:, None] * freqs[None, :]
    cos = jnp.cos(ang)[None, :, None, :]
    sin = jnp.sin(ang)[None, :, None, :]
    xf = x.astype(jnp.float32)
    x1, x2 = xf[..., :half], xf[..., half:]
    return jnp.concatenate([x1 * cos - x2 * sin, x1 * sin + x2 * cos], axis=-1).astype(x.dtype)


def causal_dwconv(buf, u, w, b):
    c = u.shape[-1]
    k = w.shape[0]
    xp = jnp.concatenate([buf.astype(u.dtype), u], axis=1)
    y = lax.conv_general_dilated(xp, w[:, None, :].astype(xp.dtype), window_strides=(1,), padding='VALID',
                                 dimension_numbers=('NWC', 'WIO', 'NWC'), feature_group_count=c)
    return y + b.astype(y.dtype), xp[:, xp.shape[1] - (k - 1):]


def mla_prompt_attend(q_lat, q_rope, kv_lat, k_rope):
    n, s = q_lat.shape[0], q_lat.shape[1]
    nb = s // Q_BLOCK
    k_pos = jnp.arange(s)

    def block(args):
        ql, qr, qpos = args
        sc = jnp.einsum('bqhr,btr->bhqt', ql, kv_lat) + jnp.einsum('bqhd,btd->bhqt', qr, k_rope)
        sc = sc.astype(jnp.float32) * ATTN_SCALE
        sc = jnp.where(k_pos[None, None, None, :] <= qpos[None, None, :, None], sc, -jnp.inf)
        pr = jax.nn.softmax(sc, axis=-1).astype(kv_lat.dtype)
        return jnp.einsum('bhqt,btr->bqhr', pr, kv_lat)

    qlb = q_lat.reshape(n, nb, Q_BLOCK, N_HEADS, KV_RANK).transpose(1, 0, 2, 3, 4)
    qrb = q_rope.reshape(n, nb, Q_BLOCK, N_HEADS, QK_ROPE).transpose(1, 0, 2, 3, 4)
    qpb = jnp.arange(s).reshape(nb, Q_BLOCK)
    o = lax.map(block, (qlb, qrb, qpb))
    return o.transpose(1, 0, 2, 3, 4).reshape(n, s, N_HEADS, KV_RANK)


def mla_sample_attend(q_lat, q_rope, kv_lat, k_rope, cache_lat, cache_rope, page_table):
    n, t = q_lat.shape[0], q_lat.shape[1]
    past_lat = cache_lat[page_table].reshape(n, -1, KV_RANK)
    past_rope = cache_rope[page_table].reshape(n, -1, QK_ROPE)
    p_len = past_lat.shape[1]
    s_past = jnp.einsum('bqhr,btr->bhqt', q_lat, past_lat.astype(q_lat.dtype)) + \
        jnp.einsum('bqhd,btd->bhqt', q_rope, past_rope.astype(q_rope.dtype))
    s_new = jnp.einsum('bqhr,btr->bhqt', q_lat, kv_lat) + jnp.einsum('bqhd,btd->bhqt', q_rope, k_rope)
    causal = jnp.arange(t)[None, :] <= jnp.arange(t)[:, None]
    s_new = jnp.where(causal[None, None], s_new.astype(jnp.float32), -jnp.inf)
    sc = jnp.concatenate([s_past.astype(jnp.float32), s_new], axis=-1) * ATTN_SCALE
    pr = jax.nn.softmax(sc, axis=-1).astype(kv_lat.dtype)
    return jnp.einsum('bhqt,btr->bqhr', pr[..., :p_len], past_lat.astype(kv_lat.dtype)) + \
        jnp.einsum('bhqt,btr->bqhr', pr[..., p_len:], kv_lat)


def decoder_layer(x, c, pos, attend, conv_buf, ffn_buf,
                  w_ada, b_ada, norm1, w_in, q_norm, w_uq, kv_norm, w_uk, w_uv,
                  conv_w, conv_b, conv_ln_g, conv_ln_b, out_norm_attn, out_norm_conv, w_out,
                  norm2, w_up, ffn_conv_w, ffn_conv_b, w_down):
    n, s = x.shape[0], x.shape[1]
    mod = jax.nn.silu(c) @ w_ada + b_ada
    sh1, sc1, g1, sh2, sc2, g2 = [m[:, None, :] for m in jnp.split(mod, N_MOD, axis=-1)]

    h = rmsnorm(x, norm1) * (1 + sc1) + sh1
    z = h @ w_in
    o1 = Q_RANK
    o2 = o1 + KV_RANK
    o3 = o2 + QK_ROPE
    c_q, c_kv, k_r, u = z[..., :o1], z[..., o1:o2], z[..., o2:o3], z[..., o3:]
    q = (rmsnorm(c_q, q_norm) @ w_uq).reshape(n, s, N_HEADS, QK_NOPE + QK_ROPE)
    q_rope = rope(q[..., QK_NOPE:], pos)
    q_lat = jnp.einsum('bshd,rhd->bshr', q[..., :QK_NOPE], w_uk)
    kv_lat = rmsnorm(c_kv, kv_norm)
    k_rope = rope(k_r[:, :, None, :], pos)[:, :, 0, :]
    o_lat = attend(q_lat, q_rope, kv_lat, k_rope)
    attn = jnp.einsum('bshr,rhv->bshv', o_lat, w_uv).reshape(n, s, ATTN_WIDTH)
    glu = u[..., :CONV_WIDTH] * jax.nn.sigmoid(u[..., CONV_WIDTH:])
    cv, conv_buf_new = causal_dwconv(conv_buf, glu, conv_w, conv_b)
    cv = jax.nn.silu(layernorm(cv, conv_ln_g, conv_ln_b))
    mix = jnp.concatenate([rmsnorm(attn, out_norm_attn), rmsnorm(cv, out_norm_conv)], axis=-1) @ w_out
    x = x + g1 * mix

    h = rmsnorm(x, norm2) * (1 + sc2) + sh2
    up, ffn_buf_new = causal_dwconv(ffn_buf, h @ w_up, ffn_conv_w, ffn_conv_b)
    ffn = (jax.nn.silu(up[..., :D_FF]) * up[..., D_FF:]) @ w_down
    x = x + g2 * ffn
    return x, kv_lat, k_rope, conv_buf_new, ffn_buf_new


def setup_inputs(seed: int = 0) -> dict:
    key = jax.random.key(seed)
    ks = iter(jax.random.split(key, 48))

    def nrm(shape, scale):
        return jax.random.normal(next(ks), shape, jnp.float32) * scale

    def gain(shape):
        return 1.0 + nrm(shape, 0.02)

    n_pages = PAST_LEN // PAGE_SIZE
    n_used = DEC_BATCH * n_pages
    n_pool = n_used + n_used // 4
    return {
        'x_prompt': nrm((BATCH, SEQ, D_MODEL), 1.0),
        'x_sample': nrm((DEC_BATCH, DEC_SEQ, D_MODEL), 1.0),
        'c_prompt': nrm((BATCH, D_MODEL), 1.0),
        'c_sample': nrm((DEC_BATCH, D_MODEL), 1.0),
        'cache_kv_latent': nrm((DEPTH, n_pool, PAGE_SIZE, KV_RANK), 1.0),
        'cache_k_rope': nrm((DEPTH, n_pool, PAGE_SIZE, QK_ROPE), 1.0),
        'state_conv': nrm((DEPTH, DEC_BATCH, CONV_K - 1, CONV_WIDTH), 0.5),
        'state_ffn_conv': nrm((DEPTH, DEC_BATCH, FFN_K - 1, 2 * D_FF), 1.0),
        'page_table': jax.random.permutation(next(ks), n_pool)[:n_used].reshape(DEC_BATCH, n_pages).astype(jnp.int32),
        'w_ada': nrm((DEPTH, D_MODEL, N_MOD * D_MODEL), 0.5 * D_MODEL ** -0.5),
        'b_ada': nrm((DEPTH, N_MOD * D_MODEL), 0.02),
        'norm1': gain((DEPTH, D_MODEL)),
        'w_in': nrm((DEPTH, D_MODEL, D_IN), D_MODEL ** -0.5),
        'q_norm': gain((DEPTH, Q_RANK)),
        'w_uq': nrm((DEPTH, Q_RANK, N_HEADS * (QK_NOPE + QK_ROPE)), Q_RANK ** -0.5),
        'kv_norm': gain((DEPTH, KV_RANK)),
        'w_uk': nrm((DEPTH, KV_RANK, N_HEADS, QK_NOPE), KV_RANK ** -0.5),
        'w_uv': nrm((DEPTH, KV_RANK, N_HEADS, V_DIM), KV_RANK ** -0.5),
        'conv_w': nrm((DEPTH, CONV_K, CONV_WIDTH), CONV_K ** -0.5),
        'conv_b': nrm((DEPTH, CONV_WIDTH), 0.02),
        'conv_ln_g': gain((DEPTH, CONV_WIDTH)),
        'conv_ln_b': nrm((DEPTH, CONV_WIDTH), 0.02),
        'out_norm_attn': gain((DEPTH, ATTN_WIDTH)),
        'out_norm_conv': gain((DEPTH, CONV_WIDTH)),
        'w_out': nrm((DEPTH, D_MIX, D_MODEL), D_MIX ** -0.5),
        'norm2': gain((DEPTH, D_MODEL)),
        'w_up': nrm((DEPTH, D_MODEL, 2 * D_FF), D_MODEL ** -0.5),
        'ffn_conv_w': nrm((DEPTH, FFN_K, 2 * D_FF), FFN_K ** -0.5),
        'ffn_conv_b': nrm((DEPTH, 2 * D_FF), 0.02),
        'w_down': nrm((DEPTH, D_FF, D_MODEL), D_FF ** -0.5),
        'final_norm': gain((D_MODEL,)),
    }


def reference(x_prompt, x_sample, c_prompt, c_sample, cache_kv_latent, cache_k_rope, state_conv,
              state_ffn_conv, page_table, w_ada, b_ada, norm1, w_in, q_norm, w_uq, kv_norm, w_uk, w_uv,
              conv_w, conv_b, conv_ln_g, conv_ln_b, out_norm_attn, out_norm_conv, w_out, norm2, w_up,
              ffn_conv_w, ffn_conv_b, w_down, final_norm):
    n_p, s_p = x_prompt.shape[0], x_prompt.shape[1]
    s_s = x_sample.shape[1]
    pos_p = jnp.arange(s_p)
    pos_s = PAST_LEN + jnp.arange(s_s)
    xp, xs = x_prompt, x_sample
    kvp_l, krp_l, cvp_l, ffp_l = [], [], [], []
    kvs_l, krs_l, cvs_l, ffs_l = [], [], [], []
    for l in range(DEPTH):
        lw = (w_ada[l], b_ada[l], norm1[l], w_in[l], q_norm[l], w_uq[l], kv_norm[l], w_uk[l], w_uv[l],
              conv_w[l], conv_b[l], conv_ln_g[l], conv_ln_b[l], out_norm_attn[l], out_norm_conv[l], w_out[l],
              norm2[l], w_up[l], ffn_conv_w[l], ffn_conv_b[l], w_down[l])
        conv0 = jnp.zeros((n_p, CONV_K - 1, CONV_WIDTH), x_prompt.dtype)
        ffn0 = jnp.zeros((n_p, FFN_K - 1, 2 * D_FF), x_prompt.dtype)
        xp, kvp, krp, cvp, ffp = decoder_layer(xp, c_prompt, pos_p, mla_prompt_attend, conv0, ffn0, *lw)
        attend_s = functools.partial(mla_sample_attend, cache_lat=cache_kv_latent[l],
                                     cache_rope=cache_k_rope[l], page_table=page_table)
        xs, kvs, krs, cvs, ffs = decoder_layer(xs, c_sample, pos_s, attend_s, state_conv[l], state_ffn_conv[l], *lw)
        kvp_l.append(kvp); krp_l.append(krp); cvp_l.append(cvp); ffp_l.append(ffp)
        kvs_l.append(kvs); krs_l.append(krs); cvs_l.append(cvs); ffs_l.append(ffs)
    y_prompt = rmsnorm(xp, final_norm)
    y_sample = rmsnorm(xs, final_norm)
    kv_latent_prompt = jnp.stack(kvp_l, 0)
    k_rope_prompt = jnp.stack(krp_l, 0)
    conv_state_prompt = jnp.stack(cvp_l, 0)
    ffn_state_prompt = jnp.stack(ffp_l, 0)
    kv_latent_sample = jnp.stack(kvs_l, 0)
    k_rope_sample = jnp.stack(krs_l, 0)
    conv_state_sample = jnp.stack(cvs_l, 0)
    ffn_state_sample = jnp.stack(ffs_l, 0)
    return (y_prompt, y_sample, kv_latent_prompt, k_rope_prompt, conv_state_prompt, ffn_state_prompt,
            kv_latent_sample, k_rope_sample, conv_state_sample, ffn_state_sample)
```

```python
import functools
import math

import jax
import jax.numpy as jnp
from jax import lax
from jax.experimental import pallas as pl
from jax.experimental.pallas import tpu as pltpu

D_MODEL = 1024
N_HEADS = 8
QK_NOPE = 64
QK_ROPE = 32
V_DIM = 64
Q_RANK = 384
KV_RANK = 256
ATTN_WIDTH = N_HEADS * V_DIM
CONV_WIDTH = 512
CONV_K = 31
D_FF = 2816
FFN_K = 3
N_MOD = 6
ROPE_THETA = 10000.0
EPS = 1e-6
PAST_LEN = 16384
PAGE_SIZE = 128
ATTN_SCALE = (QK_NOPE + QK_ROPE) ** -0.5
LOG2E = math.log2(math.e)

LANES = 128
SUBLANES = 8
HEAD_PAD = LANES
ROPE_HALF = QK_ROPE // 2
NEG_BIG = -1e30

TM_PRE = 512
TM_POST = 512
FF_CHUNK = 1408
TQ = 512
PAGES_PER_STEP = 16
CARRY_ROWS = 32
VMEM_LIMIT = 56 * 1024 * 1024

Z_CQ = 0
Z_CKV = Z_CQ + Q_RANK
Z_GA = Z_CKV + KV_RANK
Z_GB = Z_GA + CONV_WIDTH
Z_KR = Z_GB + CONV_WIDTH
Z_KRR = Z_KR + HEAD_PAD
Z_END = Z_KRR + HEAD_PAD

BF16 = jnp.bfloat16
F32 = jnp.float32


def _dot(a, b):
    return jnp.dot(a, b, preferred_element_type=F32)


def _dot_nt(a, b):
    return lax.dot_general(a, b, (((1,), (1,)), ((), ())), preferred_element_type=F32)


def _rms(x, g):
    return x * lax.rsqrt(jnp.mean(x * x, axis=-1, keepdims=True) + EPS) * g


def _silu(x):
    return x * jax.nn.sigmoid(x)


def _mod_kernel(c_ref, w_ref, b_ref, o_ref):
    s = _silu(c_ref[...])
    o_ref[...] = _dot(s.astype(BF16), w_ref[...].astype(BF16)) + b_ref[...]


def _modulation(c, w_ada, b_ada):
    n = c.shape[0]
    tn = 1024
    return pl.pallas_call(
        _mod_kernel,
        grid=(N_MOD * D_MODEL // tn,),
        in_specs=[
            pl.BlockSpec((n, D_MODEL), lambda j: (0, 0)),
            pl.BlockSpec((D_MODEL, tn), lambda j: (0, j)),
            pl.BlockSpec((1, tn), lambda j: (0, j)),
        ],
        out_specs=pl.BlockSpec((n, tn), lambda j: (0, j)),
        out_shape=jax.ShapeDtypeStruct((n, N_MOD * D_MODEL), F32),
        compiler_params=pltpu.CompilerParams(dimension_semantics=("arbitrary",), vmem_limit_bytes=VMEM_LIMIT),
        name="adaln_mod",
    )(c, w_ada, b_ada.reshape(1, -1))


def _front(x, sh1, sc1, norm1, wz_ref, qn, wq2_ref, kvn, cosq, sinq, cosk, sink):
    h = _rms(x, norm1) * (1.0 + sc1) + sh1
    z = _dot(h.astype(BF16), wz_ref[...])
    cqn = _rms(z[:, Z_CQ:Z_CKV], qn)
    q2 = _dot(cqn.astype(BF16), wq2_ref[...])
    qs = []
    for hd in range(N_HEADS):
        a = q2[:, hd * HEAD_PAD:(hd + 1) * HEAD_PAD]
        b = q2[:, (N_HEADS + hd) * HEAD_PAD:(N_HEADS + hd + 1) * HEAD_PAD]
        qs.append(a * cosq + b * sinq)
    lat = _rms(z[:, Z_CKV:Z_GA], kvn)
    kro = z[:, Z_KR:Z_KRR] * cosk + z[:, Z_KRR:Z_END] * sink
    glu = z[:, Z_GA:Z_GB] * jax.nn.sigmoid(z[:, Z_GB:Z_KR])
    return qs, lat, kro, glu


def _conv_tail(cv, lng, lnb, onc):
    mu = jnp.mean(cv, axis=-1, keepdims=True)
    d = cv - mu
    y = d * lax.rsqrt(jnp.mean(d * d, axis=-1, keepdims=True) + EPS) * lng + lnb
    return _rms(_silu(y), onc)


def _pre_kernel(x_ref, mod_ref, tab_ref, norm1_ref, wz_ref, qn_ref, wq2_ref, kvn_ref, wuk_ref, wuv_ref,
                cw_ref, cb_ref, lng_ref, lnb_ref, onc_ref,
                q_out, k_out, v_out, lat_out, kr_out, cvn_out, cst_out,
                ext_ref):
    i = pl.program_id(1)
    tm = x_ref.shape[1]
    m = mod_ref[0]
    qs, lat, kro, glu = _front(x_ref[0], m[0:1], m[1:2], norm1_ref[...], wz_ref, qn_ref[...], wq2_ref,
                               kvn_ref[...], tab_ref[0], tab_ref[1], tab_ref[2], tab_ref[3])
    for hd in range(N_HEADS):
        q_out[0, :, hd * HEAD_PAD:(hd + 1) * HEAD_PAD] = qs[hd].astype(BF16)
    lat_out[0, 0] = lat
    kr_out[0, 0] = kro[:, :QK_ROPE]
    latb = lat.astype(BF16)
    kn = _dot(latb, wuk_ref[...])
    for hd in range(N_HEADS):
        k_out[0, :, hd * HEAD_PAD:(hd + 1) * HEAD_PAD] = (kn[:, hd * HEAD_PAD:(hd + 1) * HEAD_PAD] + kro).astype(BF16)
    v_out[0] = _dot(latb, wuv_ref[...]).astype(BF16)

    @pl.when(i == 0)
    def _():
        ext_ref[0:CARRY_ROWS, :] = jnp.zeros((CARRY_ROWS, CONV_WIDTH), F32)

    ext_ref[CARRY_ROWS:CARRY_ROWS + tm, :] = glu
    base = CARRY_ROWS - (CONV_K - 1)
    cv = jnp.broadcast_to(cb_ref[...], (tm, CONV_WIDTH))
    for k in range(CONV_K):
        cv = cv + cw_ref[k:k + 1, :] * ext_ref[pl.ds(base + k, tm), :]
    ext_ref[0:CARRY_ROWS, :] = ext_ref[tm:tm + CARRY_ROWS, :]

    @pl.when(i == pl.num_programs(1) - 1)
    def _():
        cst_out[0, 0] = ext_ref[base:CARRY_ROWS, :]

    cvn_out[0] = _conv_tail(cv, lng_ref[...], lnb_ref[...], onc_ref[...]).astype(BF16)


def _const_spec(shape):
    nd = len(shape)
    return pl.BlockSpec(shape, lambda *_: (0,) * nd)


def _prompt_pre(x, mod_p, tab, norm1, wz, qn, wq2, kvn, wuk, wuv, cw, cb, lng, lnb, onc):
    n, s, _ = x.shape
    tm = TM_PRE
    grid = (n, s // tm)
    tok = lambda w: pl.BlockSpec((1, tm, w), lambda b, i: (b, i, 0))
    in_specs = [
        tok(D_MODEL),
        pl.BlockSpec((1, N_MOD, D_MODEL), lambda b, i: (b, 0, 0)),
        pl.BlockSpec((4, tm, LANES), lambda b, i: (0, i, 0)),
        _const_spec(norm1.shape), _const_spec(wz.shape), _const_spec(qn.shape), _const_spec(wq2.shape),
        _const_spec(kvn.shape), _const_spec(wuk.shape), _const_spec(wuv.shape),
        _const_spec(cw.shape), _const_spec(cb.shape), _const_spec(lng.shape), _const_spec(lnb.shape),
        _const_spec(onc.shape),
    ]
    out_shape = (
        jax.ShapeDtypeStruct((n, s, N_HEADS * HEAD_PAD), BF16),
        jax.ShapeDtypeStruct((n, s, N_HEADS * HEAD_PAD), BF16),
        jax.ShapeDtypeStruct((n, s, ATTN_WIDTH), BF16),
        jax.ShapeDtypeStruct((1, n, s, KV_RANK), F32),
        jax.ShapeDtypeStruct((1, n, s, QK_ROPE), F32),
        jax.ShapeDtypeStruct((n, s, CONV_WIDTH), BF16),
        jax.ShapeDtypeStruct((1, n, CONV_K - 1, CONV_WIDTH), F32),
    )
    out_specs = (
        tok(N_HEADS * HEAD_PAD), tok(N_HEADS * HEAD_PAD), tok(ATTN_WIDTH),
        pl.BlockSpec((1, 1, tm, KV_RANK), lambda b, i: (0, b, i, 0)),
        pl.BlockSpec((1, 1, tm, QK_ROPE), lambda b, i: (0, b, i, 0)),
        tok(CONV_WIDTH),
        pl.BlockSpec((1, 1, CONV_K - 1, CONV_WIDTH), lambda b, i: (0, b, 0, 0)),
    )
    return pl.pallas_call(
        _pre_kernel, grid=grid, in_specs=in_specs, out_specs=out_specs, out_shape=out_shape,
        scratch_shapes=[pltpu.VMEM((CARRY_ROWS + tm, CONV_WIDTH), F32)],
        compiler_params=pltpu.CompilerParams(dimension_semantics=("arbitrary", "arbitrary"),
                                             vmem_limit_bytes=VMEM_LIMIT),
        name="prompt_pre",
    )(x, mod_p, tab, norm1, wz, qn, wq2, kvn, wuk, wuv, cw, cb, lng, lnb, onc)


def _flash_kernel(qi_ref, ki_ref, q_ref, k_ref, v_ref, o_ref, m_sc, l_sc, acc_sc):
    t = pl.program_id(1)
    qi = qi_ref[t]
    ki = ki_ref[t]
    tq = q_ref.shape[1]
    tk = k_ref.shape[1]

    @pl.when(ki == 0)
    def _():
        m_sc[...] = jnp.full(m_sc.shape, NEG_BIG, F32)
        l_sc[...] = jnp.zeros(l_sc.shape, F32)
        acc_sc[...] = jnp.zeros(acc_sc.shape, F32)

    def step(masked):
        if masked:
            row = lax.broadcasted_iota(jnp.int32, (tq, tk), 0)
            col = lax.broadcasted_iota(jnp.int32, (tq, tk), 1)
            keep = col <= row
        for hd in range(N_HEADS):
            q = q_ref[0, :, hd * HEAD_PAD:(hd + 1) * HEAD_PAD]
            k = k_ref[0, :, hd * HEAD_PAD:(hd + 1) * HEAD_PAD]
            v = v_ref[0, :, hd * V_DIM:(hd + 1) * V_DIM]
            s = _dot_nt(q, k)
            if masked:
                s = jnp.where(keep, s, NEG_BIG)
            m_prev = m_sc[hd]
            m_next = jnp.maximum(m_prev, jnp.max(s, axis=1, keepdims=True))
            alpha = jnp.exp2(m_prev - m_next)
            p = jnp.exp2(s - m_next[:, 0:1])
            l_sc[hd] = alpha * l_sc[hd] + jnp.sum(p, axis=1, keepdims=True)
            acc_sc[hd] = acc_sc[hd] * alpha[:, 0:V_DIM] + _dot(p.astype(BF16), v)
            m_sc[hd] = m_next

    @pl.when(ki < qi)
    def _():
        step(False)

    @pl.when(ki == qi)
    def _():
        step(True)
        for hd in range(N_HEADS):
            o_ref[0, :, hd * V_DIM:(hd + 1) * V_DIM] = acc_sc[hd] / l_sc[hd][:, 0:V_DIM]


def _flash(q, k, v):
    n, s, _ = q.shape
    tq = TQ
    nq = s // tq
    pairs = [(a, b) for a in range(nq) for b in range(a + 1)]
    qi = jnp.asarray([p[0] for p in pairs], jnp.int32)
    ki = jnp.asarray([p[1] for p in pairs], jnp.int32)
    grid_spec = pltpu.PrefetchScalarGridSpec(
        num_scalar_prefetch=2,
        grid=(n, len(pairs)),
        in_specs=[
            pl.BlockSpec((1, tq, N_HEADS * HEAD_PAD), lambda b, t, qi, ki: (b, qi[t], 0)),
            pl.BlockSpec((1, tq, N_HEADS * HEAD_PAD), lambda b, t, qi, ki: (b, ki[t], 0)),
            pl.BlockSpec((1, tq, ATTN_WIDTH), lambda b, t, qi, ki: (b, ki[t], 0)),
        ],
        out_specs=pl.BlockSpec((1, tq, ATTN_WIDTH), lambda b, t, qi, ki: (b, qi[t], 0)),
        scratch_shapes=[
            pltpu.VMEM((N_HEADS, tq, LANES), F32),
            pltpu.VMEM((N_HEADS, tq, LANES), F32),
            pltpu.VMEM((N_HEADS, tq, V_DIM), F32),
        ],
    )
    return pl.pallas_call(
        _flash_kernel, grid_spec=grid_spec,
        out_shape=jax.ShapeDtypeStruct((n, s, ATTN_WIDTH), F32),
        compiler_params=pltpu.CompilerParams(dimension_semantics=("arbitrary", "arbitrary"),
                                             vmem_limit_bytes=VMEM_LIMIT),
        name="prompt_flash",
    )(qi, ki, q, k, v)


def _mix_residual(x, an_b, cvn_b, g1, wout_ref):
    mix = _dot(an_b, wout_ref[0:ATTN_WIDTH, :]) + _dot(cvn_b, wout_ref[ATTN_WIDTH:, :])
    return x + g1 * mix


def _post_kernel(x_ref, attn_ref, cvn_ref, mod_ref, ona_ref, wout_ref, norm2_ref, wup_ref, fcw_ref, fcb_ref,
                 wdn_ref, fn_ref, y_out, fst_out, ext_ref, carry_ref):
    i = pl.program_id(1)
    tm = x_ref.shape[1]
    m = mod_ref[0]
    g1, sh2, sc2, g2 = m[2:3], m[3:4], m[4:5], m[5:6]
    an = _rms(attn_ref[0], ona_ref[...])
    x1 = _mix_residual(x_ref[0], an.astype(BF16), cvn_ref[0], g1, wout_ref)
    h2 = (_rms(x1, norm2_ref[...]) * (1.0 + sc2) + sh2).astype(BF16)

    @pl.when(i == 0)
    def _():
        carry_ref[...] = jnp.zeros(carry_ref.shape, F32)

    def conv_part(off):
        u = _dot(h2, wup_ref[:, off:off + FF_CHUNK])
        ext_ref[0:SUBLANES, :] = carry_ref[:, off:off + FF_CHUNK]
        ext_ref[SUBLANES:SUBLANES + tm, :] = u
        carry_ref[:, off:off + FF_CHUNK] = ext_ref[tm:tm + SUBLANES, :]
        up = fcb_ref[:, off:off + FF_CHUNK] + fcw_ref[2:3, off:off + FF_CHUNK] * u
        for k in range(FFN_K - 1):
            up = up + fcw_ref[k:k + 1, off:off + FF_CHUNK] * ext_ref[pl.ds(SUBLANES - (FFN_K - 1) + k, tm), :]
        return up

    ffn = jnp.zeros((tm, D_MODEL), F32)
    for c in range(D_FF // FF_CHUNK):
        a = conv_part(c * FF_CHUNK)
        v = conv_part(D_FF + c * FF_CHUNK)
        act = (_silu(a) * v).astype(BF16)
        ffn = ffn + _dot(act, wdn_ref[c * FF_CHUNK:(c + 1) * FF_CHUNK, :])
    x2 = x1 + g2 * ffn
    y_out[0] = _rms(x2, fn_ref[...])

    @pl.when(i == pl.num_programs(1) - 1)
    def _():
        fst_out[0, 0] = carry_ref[SUBLANES - (FFN_K - 1):SUBLANES, :]


def _prompt_post(x, attn, cvn, mod_p, ona, wout, norm2, wup, fcw, fcb, wdn, fnorm):
    n, s, _ = x.shape
    tm = TM_POST
    tok = lambda w: pl.BlockSpec((1, tm, w), lambda b, i: (b, i, 0))
    in_specs = [
        tok(D_MODEL), tok(ATTN_WIDTH), tok(CONV_WIDTH),
        pl.BlockSpec((1, N_MOD, D_MODEL), lambda b, i: (b, 0, 0)),
        _const_spec(ona.shape), _const_spec(wout.shape), _const_spec(norm2.shape), _const_spec(wup.shape),
        _const_spec(fcw.shape), _const_spec(fcb.shape), _const_spec(wdn.shape), _const_spec(fnorm.shape),
    ]
    out_shape = (
        jax.ShapeDtypeStruct((n, s, D_MODEL), F32),
        jax.ShapeDtypeStruct((1, n, FFN_K - 1, 2 * D_FF), F32),
    )
    out_specs = (
        tok(D_MODEL),
        pl.BlockSpec((1, 1, FFN_K - 1, 2 * D_FF), lambda b, i: (0, b, 0, 0)),
    )
    return pl.pallas_call(
        _post_kernel, grid=(n, s // tm), in_specs=in_specs, out_specs=out_specs, out_shape=out_shape,
        scratch_shapes=[pltpu.VMEM((SUBLANES + tm, FF_CHUNK), F32), pltpu.VMEM((SUBLANES, 2 * D_FF), F32)],
        compiler_params=pltpu.CompilerParams(dimension_semantics=("arbitrary", "arbitrary"),
                                             vmem_limit_bytes=VMEM_LIMIT),
        name="prompt_post",
    )(x, attn, cvn, mod_p, ona, wout, norm2, wup, fcw, fcb, wdn, fnorm)


def _spre_kernel(x_ref, mod_ref, tab_ref, norm1_ref, wz_ref, qn_ref, wq2_ref, kvn_ref, wukt_ref,
                 cw_ref, cb_ref, lng_ref, lnb_ref, onc_ref, st_ref,
                 qlat_out, q_out, lat_out, kr_out, cvn_out, st_out):
    mod = mod_ref[...]
    sh1 = mod[:, 0:D_MODEL]
    sc1 = mod[:, D_MODEL:2 * D_MODEL]
    qs, lat, kro, glu = _front(x_ref[...], sh1, sc1, norm1_ref[...], wz_ref, qn_ref[...], wq2_ref,
                               kvn_ref[...], tab_ref[0], tab_ref[1], tab_ref[2], tab_ref[3])
    q = jnp.concatenate(qs, axis=1).astype(BF16)
    q_out[...] = q
    qlat_out[...] = _dot(q, wukt_ref[...]).astype(BF16)
    lat_out[...] = lat
    kr_out[...] = kro[:, :QK_ROPE]
    kst = CONV_K - 1
    cv = cb_ref[...] + cw_ref[kst:kst + 1, :] * glu
    for k in range(kst):
        cv = cv + cw_ref[k:k + 1, :] * st_ref[:, k * CONV_WIDTH:(k + 1) * CONV_WIDTH]
    st_out[:, 0:(kst - 1) * CONV_WIDTH] = st_ref[:, CONV_WIDTH:kst * CONV_WIDTH]
    st_out[:, (kst - 1) * CONV_WIDTH:kst * CONV_WIDTH] = glu
    cvn_out[...] = _conv_tail(cv, lng_ref[...], lnb_ref[...], onc_ref[...]).astype(BF16)


def _sample_pre(x, mod_s, tab, norm1, wz, qn, wq2, kvn, wukt, cw, cb, lng, lnb, onc, st):
    b = x.shape[0]
    out_shape = (
        jax.ShapeDtypeStruct((b, N_HEADS * KV_RANK), BF16),
        jax.ShapeDtypeStruct((b, N_HEADS * HEAD_PAD), BF16),
        jax.ShapeDtypeStruct((b, KV_RANK), F32),
        jax.ShapeDtypeStruct((b, QK_ROPE), F32),
        jax.ShapeDtypeStruct((b, CONV_WIDTH), BF16),
        jax.ShapeDtypeStruct(st.shape, F32),
    )
    return pl.pallas_call(
        _spre_kernel, out_shape=out_shape,
        compiler_params=pltpu.CompilerParams(vmem_limit_bytes=VMEM_LIMIT),
        name="sample_pre",
    )(x, mod_s, tab, norm1, wz, qn, wq2, kvn, wukt, cw, cb, lng, lnb, onc, st)


def _decode_kernel(pt_ref, qlat_ref, qrope_ref, latn_ref, ropen_ref, *refs):
    g_pages = PAGES_PER_STEP
    lat_refs = refs[:g_pages]
    rope_refs = refs[g_pages:2 * g_pages]
    o_ref, m_sc, l_sc, acc_sc = refs[2 * g_pages:]
    j = pl.program_id(1)

    @pl.when(j == 0)
    def _():
        m_sc[...] = jnp.full(m_sc.shape, NEG_BIG, F32)
        l_sc[...] = jnp.zeros(l_sc.shape, F32)
        acc_sc[...] = jnp.zeros(acc_sc.shape, F32)

    ql = qlat_ref[0]
    qr = qrope_ref[0]
    lats = []
    ss = []
    for g in range(g_pages):
        lat = lat_refs[g][0, 0].astype(BF16)
        rp = rope_refs[g][0, 0].astype(BF16)
        lats.append(lat)
        ss.append(_dot_nt(ql, lat) + _dot_nt(qr, rp))
    s = jnp.concatenate(ss, axis=1)
    m_prev = m_sc[...]
    m_next = jnp.maximum(m_prev, jnp.max(s, axis=1, keepdims=True))
    alpha = jnp.exp2(m_prev - m_next)
    p = jnp.exp2(s - m_next[:, 0:1])
    l_sc[...] = alpha * l_sc[...] + jnp.sum(p, axis=1, keepdims=True)
    pb = p.astype(BF16)
    pv = _dot(pb[:, 0:PAGE_SIZE], lats[0])
    for g in range(1, g_pages):
        pv = pv + _dot(pb[:, g * PAGE_SIZE:(g + 1) * PAGE_SIZE], lats[g])
    acc_sc[...] = acc_sc[...] * alpha[:, 0:1] + pv
    m_sc[...] = m_next

    @pl.when(j == pl.num_programs(1) - 1)
    def _():
        qlf = ql.astype(F32)
        qrf = qr.astype(F32)
        s_new = (jnp.sum(qlf * latn_ref[0], axis=1, keepdims=True)
                 + jnp.sum(qrf * ropen_ref[0], axis=1, keepdims=True))
        m_old = m_sc[...][:, 0:1]
        m_fin = jnp.maximum(m_old, s_new)
        a_old = jnp.exp2(m_old - m_fin)
        p_new = jnp.exp2(s_new - m_fin)
        l_fin = a_old * l_sc[...][:, 0:1] + p_new
        o_ref[0] = (acc_sc[...] * a_old + p_new * latn_ref[0]) / l_fin


def _decode(page_table, qlat, qrope, lat_new, rope_new, cache_lat, cache_rope):
    b = qlat.shape[0]
    n_pages = page_table.shape[1]
    g_pages = PAGES_PER_STEP
    in_specs = [
        pl.BlockSpec((1, N_HEADS, KV_RANK), lambda s, j, pt: (s, 0, 0)),
        pl.BlockSpec((1, N_HEADS, QK_ROPE), lambda s, j, pt: (s, 0, 0)),
        pl.BlockSpec((1, 1, KV_RANK), lambda s, j, pt: (s, 0, 0)),
        pl.BlockSpec((1, 1, QK_ROPE), lambda s, j, pt: (s, 0, 0)),
    ]
    for g in range(g_pages):
        in_specs.append(pl.BlockSpec((1, 1, PAGE_SIZE, KV_RANK),
                                     lambda s, j, pt, g=g: (0, pt[s, j * g_pages + g], 0, 0)))
    for g in range(g_pages):
        in_specs.append(pl.BlockSpec((1, 1, PAGE_SIZE, QK_ROPE),
                                     lambda s, j, pt, g=g: (0, pt[s, j * g_pages + g], 0, 0)))
    grid_spec = pltpu.PrefetchScalarGridSpec(
        num_scalar_prefetch=1,
        grid=(b, n_pages // g_pages),
        in_specs=in_specs,
        out_specs=pl.BlockSpec((1, N_HEADS, KV_RANK), lambda s, j, pt: (s, 0, 0)),
        scratch_shapes=[
            pltpu.VMEM((N_HEADS, LANES), F32),
            pltpu.VMEM((N_HEADS, LANES), F32),
            pltpu.VMEM((N_HEADS, KV_RANK), F32),
        ],
    )
    return pl.pallas_call(
        _decode_kernel, grid_spec=grid_spec,
        out_shape=jax.ShapeDtypeStruct((b, N_HEADS, KV_RANK), F32),
        compiler_params=pltpu.CompilerParams(dimension_semantics=("arbitrary", "arbitrary"),
                                             vmem_limit_bytes=VMEM_LIMIT),
        name="sample_decode",
    )(page_table, qlat, qrope, lat_new, rope_new, *([cache_lat] * g_pages), *([cache_rope] * g_pages))


def _spost_kernel(x_ref, olat_ref, cvn_ref, mod_ref, wuvbd_ref, ona_ref, wout_ref, norm2_ref, wup_ref,
                  fcw_ref, fcb_ref, wdn_ref, fn_ref, st_ref, y_out, st_out):
    mod = mod_ref[...]
    g1 = mod[:, 2 * D_MODEL:3 * D_MODEL]
    sh2 = mod[:, 3 * D_MODEL:4 * D_MODEL]
    sc2 = mod[:, 4 * D_MODEL:5 * D_MODEL]
    g2 = mod[:, 5 * D_MODEL:6 * D_MODEL]
    attn = _dot(olat_ref[...].astype(BF16), wuvbd_ref[...])
    an = _rms(attn, ona_ref[...])
    x1 = _mix_residual(x_ref[...], an.astype(BF16), cvn_ref[...], g1, wout_ref)
    h2 = (_rms(x1, norm2_ref[...]) * (1.0 + sc2) + sh2).astype(BF16)
    u = _dot(h2, wup_ref[...])
    w = 2 * D_FF
    up = fcb_ref[...] + fcw_ref[0:1, :] * st_ref[:, 0:w] + fcw_ref[1:2, :] * st_ref[:, w:2 * w] + fcw_ref[2:3, :] * u
    st_out[:, 0:w] = st_ref[:, w:2 * w]
    st_out[:, w:2 * w] = u
    act = (_silu(up[:, 0:D_FF]) * up[:, D_FF:w]).astype(BF16)
    x2 = x1 + g2 * _dot(act, wdn_ref[...])
    y_out[...] = _rms(x2, fn_ref[...])


def _sample_post(x, olat, cvn, mod_s, wuvbd, ona, wout, norm2, wup, fcw, fcb, wdn, fnorm, st):
    b = x.shape[0]
    out_shape = (jax.ShapeDtypeStruct((b, D_MODEL), F32), jax.ShapeDtypeStruct(st.shape, F32))
    return pl.pallas_call(
        _spost_kernel, out_shape=out_shape,
        compiler_params=pltpu.CompilerParams(vmem_limit_bytes=VMEM_LIMIT),
        name="sample_post",
    )(x, olat, cvn, mod_s, wuvbd, ona, wout, norm2, wup, fcw, fcb, wdn, fnorm, st)


def _rope_tables(pos):
    freqs = ROPE_THETA ** (-jnp.arange(ROPE_HALF, dtype=F32) / ROPE_HALF)
    ang = pos.astype(F32)[:, None] * freqs[None, :]
    cos = jnp.cos(ang)
    sin = jnp.sin(ang)
    s = pos.shape[0]
    c2 = jnp.concatenate([cos, cos], axis=1)
    s2 = jnp.concatenate([sin, sin], axis=1)
    zpad = jnp.zeros((s, HEAD_PAD - QK_ROPE), F32)
    qscale = ATTN_SCALE * LOG2E
    cosq = jnp.concatenate([c2, jnp.ones((s, QK_NOPE), F32), jnp.zeros((s, HEAD_PAD - QK_ROPE - QK_NOPE), F32)],
                           axis=1) * qscale
    sinq = jnp.concatenate([s2, zpad], axis=1) * qscale
    cosk = jnp.concatenate([c2, zpad], axis=1)
    sink = jnp.concatenate([s2, zpad], axis=1)
    return jnp.stack([cosq, sinq, cosk, sink], axis=0)


def _rot_half_cols(w):
    return jnp.concatenate([-w[..., ROPE_HALF:], w[..., :ROPE_HALF]], axis=-1)


def _pack_weights(w_in, w_uq, w_uk, w_uv):
    o1, o2, o3 = Q_RANK, Q_RANK + KV_RANK, Q_RANK + KV_RANK + QK_ROPE
    wkr = w_in[:, o2:o3]
    zpad = jnp.zeros((D_MODEL, HEAD_PAD - QK_ROPE), F32)
    wz = jnp.concatenate([w_in[:, :o2], w_in[:, o3:], wkr, zpad, _rot_half_cols(wkr), zpad], axis=1).astype(BF16)

    wq = w_uq.reshape(Q_RANK, N_HEADS, QK_NOPE + QK_ROPE)
    wq_nope, wq_rope = wq[..., :QK_NOPE], wq[..., QK_NOPE:]
    ztail = jnp.zeros((Q_RANK, N_HEADS, HEAD_PAD - QK_ROPE - QK_NOPE), F32)
    wq_plain = jnp.concatenate([wq_rope, wq_nope, ztail], axis=2).reshape(Q_RANK, N_HEADS * HEAD_PAD)
    wq_rot = jnp.concatenate([_rot_half_cols(wq_rope), jnp.zeros((Q_RANK, N_HEADS, HEAD_PAD - QK_ROPE), F32)],
                             axis=2).reshape(Q_RANK, N_HEADS * HEAD_PAD)
    wq2 = jnp.concatenate([wq_plain, wq_rot], axis=1).astype(BF16)

    wuk = jnp.concatenate([jnp.zeros((KV_RANK, N_HEADS, QK_ROPE), F32), w_uk,
                           jnp.zeros((KV_RANK, N_HEADS, HEAD_PAD - QK_ROPE - QK_NOPE), F32)],
                          axis=2).reshape(KV_RANK, N_HEADS * HEAD_PAD).astype(BF16)
    wuv = w_uv.reshape(KV_RANK, ATTN_WIDTH).astype(BF16)

    wukt = jnp.transpose(w_uk, (1, 2, 0))
    wukt = jnp.concatenate([jnp.zeros((N_HEADS, QK_ROPE, KV_RANK), F32), wukt,
                            jnp.zeros((N_HEADS, HEAD_PAD - QK_ROPE - QK_NOPE, KV_RANK), F32)], axis=1)
    eye = jnp.eye(N_HEADS, dtype=F32)
    wukt_bd = (eye[:, None, :, None] * wukt[:, :, None, :]).reshape(N_HEADS * HEAD_PAD, N_HEADS * KV_RANK).astype(BF16)
    wuvh = jnp.transpose(w_uv, (1, 0, 2))
    wuv_bd = (eye[:, None, :, None] * wuvh[:, :, None, :]).reshape(N_HEADS * KV_RANK, ATTN_WIDTH).astype(BF16)
    return wz, wq2, wuk, wuv, wukt_bd, wuv_bd


def kernel(x_prompt, x_sample, c_prompt, c_sample, cache_kv_latent, cache_k_rope, state_conv, state_ffn_conv, page_table, w_ada, b_ada, norm1, w_in, q_norm, w_uq, kv_norm, w_uk, w_uv, conv_w, conv_b, conv_ln_g, conv_ln_b, out_norm_attn, out_norm_conv, w_out, norm2, w_up, ffn_conv_w, ffn_conv_b, w_down, final_norm):
    assert w_ada.shape[0] == 1, "single-layer decoder"
    n_p, s_p, _ = x_prompt.shape
    n_s, s_s, _ = x_sample.shape
    assert s_s == 1

    row = lambda a: a.reshape(1, -1)
    wz, wq2, wuk, wuv, wukt_bd, wuv_bd = _pack_weights(w_in[0], w_uq[0], w_uk[0], w_uv[0])
    wout = w_out[0].astype(BF16)
    wup = w_up[0].astype(BF16)
    wdn = w_down[0].astype(BF16)
    norm1_r, qn_r, kvn_r = row(norm1[0]), row(q_norm[0]), row(kv_norm[0])
    cw, cb = conv_w[0], row(conv_b[0])
    lng, lnb, onc, ona = row(conv_ln_g[0]), row(conv_ln_b[0]), row(out_norm_conv[0]), row(out_norm_attn[0])
    norm2_r, fcw, fcb, fnorm = row(norm2[0]), ffn_conv_w[0], row(ffn_conv_b[0]), row(final_norm)

    n_c = n_p + n_s
    n_c_pad = -(-n_c // SUBLANES) * SUBLANES
    c_all = jnp.concatenate([c_prompt, c_sample, jnp.zeros((n_c_pad - n_c, D_MODEL), F32)], axis=0)
    mod = _modulation(c_all, w_ada[0], b_ada[0])
    mod_p = mod[:n_p].reshape(n_p, N_MOD, D_MODEL)
    mod_s = mod[n_p:n_c]

    tab_p = _rope_tables(jnp.arange(s_p))
    q, k, v, kv_lat_p, k_rope_p, cvn_p, conv_st_p = _prompt_pre(
        x_prompt, mod_p, tab_p, norm1_r, wz, qn_r, wq2, kvn_r, wuk, wuv, cw, cb, lng, lnb, onc)
    attn_p = _flash(q, k, v)
    y_p, ffn_st_p = _prompt_post(x_prompt, attn_p, cvn_p, mod_p, ona, wout, norm2_r, wup, fcw, fcb, wdn, fnorm)

    tab_s = _rope_tables(PAST_LEN + jnp.arange(s_s))
    st_conv = state_conv[0].reshape(n_s, (CONV_K - 1) * CONV_WIDTH)
    qlat, q_s, lat_s, kr_s, cvn_s, st_conv_new = _sample_pre(
        x_sample[:, 0], mod_s, tab_s, norm1_r, wz, qn_r, wq2, kvn_r, wukt_bd, cw, cb, lng, lnb, onc, st_conv)
    qrope = q_s.reshape(n_s, N_HEADS, HEAD_PAD)[:, :, :QK_ROPE]
    o_lat = _decode(page_table, qlat.reshape(n_s, N_HEADS, KV_RANK), qrope,
                    lat_s.reshape(n_s, 1, KV_RANK), kr_s.reshape(n_s, 1, QK_ROPE),
                    cache_kv_latent, cache_k_rope)
    st_ffn = state_ffn_conv[0].reshape(n_s, (FFN_K - 1) * 2 * D_FF)
    y_s, st_ffn_new = _sample_post(
        x_sample[:, 0], o_lat.reshape(n_s, N_HEADS * KV_RANK), cvn_s, mod_s, wuv_bd, ona, wout, norm2_r,
        wup, fcw, fcb, wdn, fnorm, st_ffn)

    return (
        y_p,
        y_s.reshape(n_s, 1, D_MODEL),
        kv_lat_p,
        k_rope_p,
        conv_st_p,
        ffn_st_p,
        lat_s.reshape(1, n_s, 1, KV_RANK),
        kr_s.reshape(1, n_s, 1, QK_ROPE),
        st_conv_new.reshape(1, n_s, CONV_K - 1, CONV_WIDTH),
        st_ffn_new.reshape(1, n_s, FFN_K - 1, 2 * D_FF),
    )
```

```python
import functools
import math

import jax
import jax.numpy as jnp
from jax import lax
from jax.experimental import pallas as pl
from jax.experimental.pallas import tpu as pltpu

D_MODEL = 1024
N_HEADS = 8
QK_NOPE = 64
QK_ROPE = 32
V_DIM = 64
Q_RANK = 384
KV_RANK = 256
ATTN_WIDTH = N_HEADS * V_DIM
CONV_WIDTH = 512
CONV_K = 31
D_FF = 2816
FFN_K = 3
N_MOD = 6
ROPE_THETA = 10000.0
EPS = 1e-6
PAST_LEN = 16384
PAGE_SIZE = 128
ATTN_SCALE = (QK_NOPE + QK_ROPE) ** -0.5
LOG2E = math.log2(math.e)

LANES = 128
SUBLANES = 8
HEAD_PAD = LANES
ROPE_HALF = QK_ROPE // 2
NEG_BIG = -1e30

TM_PRE = 512
TM_POST = 512
FF_CHUNK = 1408
TQ = 512
PAGES_PER_STEP = 64
DECODE_CHAINS = 2
CARRY_ROWS = 32
VMEM_LIMIT = 56 * 1024 * 1024

Z_CQ = 0
Z_CKV = Z_CQ + Q_RANK
Z_GA = Z_CKV + KV_RANK
Z_GB = Z_GA + CONV_WIDTH
Z_KR = Z_GB + CONV_WIDTH
Z_KRR = Z_KR + HEAD_PAD
Z_END = Z_KRR + HEAD_PAD

BF16 = jnp.bfloat16
F32 = jnp.float32


def _dot(a, b):
    return jnp.dot(a, b, preferred_element_type=F32)


def _dot_nt(a, b):
    return lax.dot_general(a, b, (((1,), (1,)), ((), ())), preferred_element_type=F32)


def _rms(x, g):
    return x * lax.rsqrt(jnp.mean(x * x, axis=-1, keepdims=True) + EPS) * g


def _silu(x):
    return x * jax.nn.sigmoid(x)


def _mod_kernel(c_ref, w_ref, b_ref, o_ref):
    s = _silu(c_ref[...])
    o_ref[...] = _dot(s.astype(BF16), w_ref[...].astype(BF16)) + b_ref[...]


def _modulation(c, w_ada, b_ada):
    n = c.shape[0]
    tn = 1024
    return pl.pallas_call(
        _mod_kernel,
        grid=(N_MOD * D_MODEL // tn,),
        in_specs=[
            pl.BlockSpec((n, D_MODEL), lambda j: (0, 0)),
            pl.BlockSpec((D_MODEL, tn), lambda j: (0, j)),
            pl.BlockSpec((1, tn), lambda j: (0, j)),
        ],
        out_specs=pl.BlockSpec((n, tn), lambda j: (0, j)),
        out_shape=jax.ShapeDtypeStruct((n, N_MOD * D_MODEL), F32),
        compiler_params=pltpu.CompilerParams(dimension_semantics=("arbitrary",), vmem_limit_bytes=VMEM_LIMIT),
        name="adaln_mod",
    )(c, w_ada, b_ada.reshape(1, -1))


def _front(x, sh1, sc1, norm1, wz_ref, qn, kvn, cosk, sink):
    h = _rms(x, norm1) * (1.0 + sc1) + sh1
    z = _dot(h.astype(BF16), wz_ref[...])
    cqn = _rms(z[:, Z_CQ:Z_CKV], qn).astype(BF16)
    lat = _rms(z[:, Z_CKV:Z_GA], kvn)
    kro = z[:, Z_KR:Z_KRR] * cosk + z[:, Z_KRR:Z_END] * sink
    glu = z[:, Z_GA:Z_GB] * jax.nn.sigmoid(z[:, Z_GB:Z_KR])
    return cqn, lat, kro, glu


def _conv_tail(cv, lng, lnb, onc):
    mu = jnp.mean(cv, axis=-1, keepdims=True)
    d = cv - mu
    y = d * lax.rsqrt(jnp.mean(d * d, axis=-1, keepdims=True) + EPS) * lng + lnb
    return _rms(_silu(y), onc)


def _pre_kernel(x_ref, mod_ref, tabq_ref, tabk_ref, norm1_ref, wz_ref, qn_ref, wq2t_ref, kvn_ref, wuk_ref, wuvt_ref,
                cw_ref, cb_ref, lng_ref, lnb_ref, onc_ref,
                qt_out, k_out, vt_out, lat_out, kr_out, cvn_out, cst_out,
                ext_ref):
    i = pl.program_id(1)
    tm = x_ref.shape[1]
    m = mod_ref[0]
    cqn, lat, kro, glu = _front(x_ref[0], m[0:1], m[1:2], norm1_ref[...], wz_ref, qn_ref[...],
                                kvn_ref[...], tabk_ref[0], tabk_ref[1])
    q2t = _dot_nt(wq2t_ref[...], cqn)
    cosqt = tabq_ref[0]
    sinqt = tabq_ref[1]
    hw = N_HEADS * HEAD_PAD
    for hd in range(N_HEADS):
        lo = hd * HEAD_PAD
        qt_out[0, lo:lo + HEAD_PAD, :] = (q2t[lo:lo + HEAD_PAD, :] * cosqt
                                          + q2t[hw + lo:hw + lo + HEAD_PAD, :] * sinqt).astype(BF16)
    lat_out[0, 0] = lat
    kr_out[0, 0] = kro[:, :QK_ROPE]
    latb = lat.astype(BF16)
    kn = _dot(latb, wuk_ref[...])
    for hd in range(N_HEADS):
        k_out[0, :, hd * HEAD_PAD:(hd + 1) * HEAD_PAD] = (kn[:, hd * HEAD_PAD:(hd + 1) * HEAD_PAD] + kro).astype(BF16)
    vt_out[0] = _dot_nt(wuvt_ref[...], latb).astype(BF16)

    @pl.when(i == 0)
    def _():
        ext_ref[0:CARRY_ROWS, :] = jnp.zeros((CARRY_ROWS, CONV_WIDTH), F32)

    ext_ref[CARRY_ROWS:CARRY_ROWS + tm, :] = glu
    base = CARRY_ROWS - (CONV_K - 1)
    cv = jnp.broadcast_to(cb_ref[...], (tm, CONV_WIDTH))
    for k in range(CONV_K):
        cv = cv + cw_ref[k:k + 1, :] * ext_ref[pl.ds(base + k, tm), :]
    ext_ref[0:CARRY_ROWS, :] = ext_ref[tm:tm + CARRY_ROWS, :]

    @pl.when(i == pl.num_programs(1) - 1)
    def _():
        cst_out[0, 0] = ext_ref[base:CARRY_ROWS, :]

    cvn_out[0] = _conv_tail(cv, lng_ref[...], lnb_ref[...], onc_ref[...]).astype(BF16)


def _const_spec(shape):
    nd = len(shape)
    return pl.BlockSpec(shape, lambda *_: (0,) * nd)


def _prompt_pre(x, mod_p, tabq, tabk, norm1, wz, qn, wq2t, kvn, wuk, wuvt, cw, cb, lng, lnb, onc):
    n, s, _ = x.shape
    tm = TM_PRE
    grid = (n, s // tm)
    tok = lambda w: pl.BlockSpec((1, tm, w), lambda b, i: (b, i, 0))
    tok_t = lambda w: pl.BlockSpec((1, w, tm), lambda b, i: (b, 0, i))
    in_specs = [
        tok(D_MODEL),
        pl.BlockSpec((1, N_MOD, D_MODEL), lambda b, i: (b, 0, 0)),
        pl.BlockSpec((2, HEAD_PAD, tm), lambda b, i: (0, 0, i)),
        pl.BlockSpec((2, tm, LANES), lambda b, i: (0, i, 0)),
        _const_spec(norm1.shape), _const_spec(wz.shape), _const_spec(qn.shape), _const_spec(wq2t.shape),
        _const_spec(kvn.shape), _const_spec(wuk.shape), _const_spec(wuvt.shape),
        _const_spec(cw.shape), _const_spec(cb.shape), _const_spec(lng.shape), _const_spec(lnb.shape),
        _const_spec(onc.shape),
    ]
    out_shape = (
        jax.ShapeDtypeStruct((n, N_HEADS * HEAD_PAD, s), BF16),
        jax.ShapeDtypeStruct((n, s, N_HEADS * HEAD_PAD), BF16),
        jax.ShapeDtypeStruct((n, ATTN_WIDTH, s), BF16),
        jax.ShapeDtypeStruct((1, n, s, KV_RANK), F32),
        jax.ShapeDtypeStruct((1, n, s, QK_ROPE), F32),
        jax.ShapeDtypeStruct((n, s, CONV_WIDTH), BF16),
        jax.ShapeDtypeStruct((1, n, CONV_K - 1, CONV_WIDTH), F32),
    )
    out_specs = (
        tok_t(N_HEADS * HEAD_PAD), tok(N_HEADS * HEAD_PAD), tok_t(ATTN_WIDTH),
        pl.BlockSpec((1, 1, tm, KV_RANK), lambda b, i: (0, b, i, 0)),
        pl.BlockSpec((1, 1, tm, QK_ROPE), lambda b, i: (0, b, i, 0)),
        tok(CONV_WIDTH),
        pl.BlockSpec((1, 1, CONV_K - 1, CONV_WIDTH), lambda b, i: (0, b, 0, 0)),
    )
    return pl.pallas_call(
        _pre_kernel, grid=grid, in_specs=in_specs, out_specs=out_specs, out_shape=out_shape,
        scratch_shapes=[pltpu.VMEM((CARRY_ROWS + tm, CONV_WIDTH), F32)],
        compiler_params=pltpu.CompilerParams(dimension_semantics=("arbitrary", "arbitrary"),
                                             vmem_limit_bytes=VMEM_LIMIT),
        name="prompt_pre",
    )(x, mod_p, tabq, tabk, norm1, wz, qn, wq2t, kvn, wuk, wuvt, cw, cb, lng, lnb, onc)


def _flash_kernel(qi_ref, ki_ref, qt_ref, k_ref, vt_ref, o_ref, m_sc, l_sc, acc_sc):
    t = pl.program_id(1)
    qi = qi_ref[t]
    ki = ki_ref[t]
    tq = qt_ref.shape[2]
    tk = k_ref.shape[1]

    @pl.when(ki == 0)
    def _():
        m_sc[...] = jnp.full(m_sc.shape, NEG_BIG, F32)
        l_sc[...] = jnp.zeros(l_sc.shape, F32)
        acc_sc[...] = jnp.zeros(acc_sc.shape, F32)

    def scores(hd):
        qt = qt_ref[0, hd * HEAD_PAD:(hd + 1) * HEAD_PAD, :]
        k = k_ref[0, :, hd * HEAD_PAD:(hd + 1) * HEAD_PAD]
        return _dot(k, qt)

    def step(masked):
        if masked:
            key = lax.broadcasted_iota(jnp.int32, (tk, tq), 0)
            qry = lax.broadcasted_iota(jnp.int32, (tk, tq), 1)
            keep = key <= qry
        st_next = scores(0)
        for hd in range(N_HEADS):
            st = st_next
            if hd + 1 < N_HEADS:
                st_next = scores(hd + 1)
            if masked:
                st = jnp.where(keep, st, NEG_BIG)
            vt = vt_ref[0, hd * V_DIM:(hd + 1) * V_DIM, :]
            m_prev = m_sc[hd]
            m_next = jnp.maximum(m_prev, jnp.max(st, axis=0, keepdims=True))
            alpha = jnp.exp2(m_prev - m_next)
            p = jnp.exp2(st - m_next)
            l_sc[hd] = alpha * l_sc[hd] + jnp.sum(p, axis=0, keepdims=True)
            pr, lo = hd // 2, (hd % 2) * V_DIM
            acc_sc[pr, lo:lo + V_DIM, :] = acc_sc[pr, lo:lo + V_DIM, :] * alpha + _dot(vt, p.astype(BF16))
            m_sc[hd] = m_next

    @pl.when(ki < qi)
    def _():
        step(False)

    @pl.when(ki == qi)
    def _():
        step(True)
        for pr in range(N_HEADS // 2):
            inv = jnp.concatenate([jnp.broadcast_to(1.0 / l_sc[2 * pr], (V_DIM, tq)),
                                   jnp.broadcast_to(1.0 / l_sc[2 * pr + 1], (V_DIM, tq))], axis=0)
            o_ref[0, :, pr * LANES:(pr + 1) * LANES] = (acc_sc[pr] * inv).T


def _flash(qt, k, vt):
    n, s, _ = k.shape
    tq = TQ
    nq = s // tq
    pairs = [(a, b) for a in range(nq) for b in range(a + 1)]
    qi = jnp.asarray([p[0] for p in pairs], jnp.int32)
    ki = jnp.asarray([p[1] for p in pairs], jnp.int32)
    grid_spec = pltpu.PrefetchScalarGridSpec(
        num_scalar_prefetch=2,
        grid=(n, len(pairs)),
        in_specs=[
            pl.BlockSpec((1, N_HEADS * HEAD_PAD, tq), lambda b, t, qi, ki: (b, 0, qi[t])),
            pl.BlockSpec((1, tq, N_HEADS * HEAD_PAD), lambda b, t, qi, ki: (b, ki[t], 0)),
            pl.BlockSpec((1, ATTN_WIDTH, tq), lambda b, t, qi, ki: (b, 0, ki[t])),
        ],
        out_specs=pl.BlockSpec((1, tq, ATTN_WIDTH), lambda b, t, qi, ki: (b, qi[t], 0)),
        scratch_shapes=[
            pltpu.VMEM((N_HEADS, 1, tq), F32),
            pltpu.VMEM((N_HEADS, 1, tq), F32),
            pltpu.VMEM((N_HEADS // 2, 2 * V_DIM, tq), F32),
        ],
    )
    return pl.pallas_call(
        _flash_kernel, grid_spec=grid_spec,
        out_shape=jax.ShapeDtypeStruct((n, s, ATTN_WIDTH), F32),
        compiler_params=pltpu.CompilerParams(dimension_semantics=("arbitrary", "arbitrary"),
                                             vmem_limit_bytes=VMEM_LIMIT),
        name="prompt_flash",
    )(qi, ki, qt, k, vt)


def _mix_residual(x, an_b, cvn_b, g1, wout_ref):
    mix = _dot(an_b, wout_ref[0:ATTN_WIDTH, :]) + _dot(cvn_b, wout_ref[ATTN_WIDTH:, :])
    return x + g1 * mix


def _post_kernel(x_ref, attn_ref, cvn_ref, mod_ref, ona_ref, wout_ref, norm2_ref, wup_ref, fcw_ref, fcb_ref,
                 wdn_ref, fn_ref, y_out, fst_out, ext_ref, carry_ref):
    i = pl.program_id(1)
    tm = x_ref.shape[1]
    m = mod_ref[0]
    g1, sh2, sc2, g2 = m[2:3], m[3:4], m[4:5], m[5:6]
    an = _rms(attn_ref[0], ona_ref[...])
    x1 = _mix_residual(x_ref[0], an.astype(BF16), cvn_ref[0], g1, wout_ref)
    h2 = (_rms(x1, norm2_ref[...]) * (1.0 + sc2) + sh2).astype(BF16)

    @pl.when(i == 0)
    def _():
        carry_ref[...] = jnp.zeros(carry_ref.shape, F32)

    def conv_part(off):
        u = _dot(h2, wup_ref[:, off:off + FF_CHUNK])
        ext_ref[0:SUBLANES, :] = carry_ref[:, off:off + FF_CHUNK]
        ext_ref[SUBLANES:SUBLANES + tm, :] = u
        carry_ref[:, off:off + FF_CHUNK] = ext_ref[tm:tm + SUBLANES, :]
        up = fcb_ref[:, off:off + FF_CHUNK] + fcw_ref[2:3, off:off + FF_CHUNK] * u
        for k in range(FFN_K - 1):
            up = up + fcw_ref[k:k + 1, off:off + FF_CHUNK] * ext_ref[pl.ds(SUBLANES - (FFN_K - 1) + k, tm), :]
        return up

    ffn = jnp.zeros((tm, D_MODEL), F32)
    for c in range(D_FF // FF_CHUNK):
        a = conv_part(c * FF_CHUNK)
        v = conv_part(D_FF + c * FF_CHUNK)
        act = (_silu(a) * v).astype(BF16)
        ffn = ffn + _dot(act, wdn_ref[c * FF_CHUNK:(c + 1) * FF_CHUNK, :])
    x2 = x1 + g2 * ffn
    y_out[0] = _rms(x2, fn_ref[...])

    @pl.when(i == pl.num_programs(1) - 1)
    def _():
        fst_out[0, 0] = carry_ref[SUBLANES - (FFN_K - 1):SUBLANES, :]


def _prompt_post(x, attn, cvn, mod_p, ona, wout, norm2, wup, fcw, fcb, wdn, fnorm):
    n, s, _ = x.shape
    tm = TM_POST
    tok = lambda w: pl.BlockSpec((1, tm, w), lambda b, i: (b, i, 0))
    in_specs = [
        tok(D_MODEL), tok(ATTN_WIDTH), tok(CONV_WIDTH),
        pl.BlockSpec((1, N_MOD, D_MODEL), lambda b, i: (b, 0, 0)),
        _const_spec(ona.shape), _const_spec(wout.shape), _const_spec(norm2.shape), _const_spec(wup.shape),
        _const_spec(fcw.shape), _const_spec(fcb.shape), _const_spec(wdn.shape), _const_spec(fnorm.shape),
    ]
    out_shape = (
        jax.ShapeDtypeStruct((n, s, D_MODEL), F32),
        jax.ShapeDtypeStruct((1, n, FFN_K - 1, 2 * D_FF), F32),
    )
    out_specs = (
        tok(D_MODEL),
        pl.BlockSpec((1, 1, FFN_K - 1, 2 * D_FF), lambda b, i: (0, b, 0, 0)),
    )
    return pl.pallas_call(
        _post_kernel, grid=(n, s // tm), in_specs=in_specs, out_specs=out_specs, out_shape=out_shape,
        scratch_shapes=[pltpu.VMEM((SUBLANES + tm, FF_CHUNK), F32), pltpu.VMEM((SUBLANES, 2 * D_FF), F32)],
        compiler_params=pltpu.CompilerParams(dimension_semantics=("arbitrary", "arbitrary"),
                                             vmem_limit_bytes=VMEM_LIMIT),
        name="prompt_post",
    )(x, attn, cvn, mod_p, ona, wout, norm2, wup, fcw, fcb, wdn, fnorm)


def _spre_kernel(x_ref, mod_ref, tab_ref, norm1_ref, wz_ref, qn_ref, wq2_ref, kvn_ref, wukt_ref,
                 cw_ref, cb_ref, lng_ref, lnb_ref, onc_ref, st_ref,
                 qlat_out, q_out, lat_out, kr_out, cvn_out, st_out):
    mod = mod_ref[...]
    sh1 = mod[:, 0:D_MODEL]
    sc1 = mod[:, D_MODEL:2 * D_MODEL]
    cqn, lat, kro, glu = _front(x_ref[...], sh1, sc1, norm1_ref[...], wz_ref, qn_ref[...],
                                kvn_ref[...], tab_ref[2], tab_ref[3])
    q2 = _dot(cqn, wq2_ref[...])
    qs = []
    for hd in range(N_HEADS):
        a = q2[:, hd * HEAD_PAD:(hd + 1) * HEAD_PAD]
        b = q2[:, (N_HEADS + hd) * HEAD_PAD:(N_HEADS + hd + 1) * HEAD_PAD]
        qs.append(a * tab_ref[0] + b * tab_ref[1])
    q = jnp.concatenate(qs, axis=1).astype(BF16)
    q_out[...] = q
    qlat_out[...] = _dot(q, wukt_ref[...]).astype(BF16)
    lat_out[...] = lat
    kr_out[...] = kro[:, :QK_ROPE]
    kst = CONV_K - 1
    cv = cb_ref[...] + cw_ref[kst:kst + 1, :] * glu
    for k in range(kst):
        cv = cv + cw_ref[k:k + 1, :] * st_ref[:, k * CONV_WIDTH:(k + 1) * CONV_WIDTH]
    st_out[:, 0:(kst - 1) * CONV_WIDTH] = st_ref[:, CONV_WIDTH:kst * CONV_WIDTH]
    st_out[:, (kst - 1) * CONV_WIDTH:kst * CONV_WIDTH] = glu
    cvn_out[...] = _conv_tail(cv, lng_ref[...], lnb_ref[...], onc_ref[...]).astype(BF16)


def _sample_pre(x, mod_s, tab, norm1, wz, qn, wq2, kvn, wukt, cw, cb, lng, lnb, onc, st):
    b = x.shape[0]
    out_shape = (
        jax.ShapeDtypeStruct((b, N_HEADS * KV_RANK), BF16),
        jax.ShapeDtypeStruct((b, N_HEADS * HEAD_PAD), BF16),
        jax.ShapeDtypeStruct((b, KV_RANK), F32),
        jax.ShapeDtypeStruct((b, QK_ROPE), F32),
        jax.ShapeDtypeStruct((b, CONV_WIDTH), BF16),
        jax.ShapeDtypeStruct(st.shape, F32),
    )
    return pl.pallas_call(
        _spre_kernel, out_shape=out_shape,
        compiler_params=pltpu.CompilerParams(vmem_limit_bytes=VMEM_LIMIT),
        name="sample_pre",
    )(x, mod_s, tab, norm1, wz, qn, wq2, kvn, wukt, cw, cb, lng, lnb, onc, st)


def _page_copies(pt_ref, lat_hbm, rope_hbm, lat_buf, rope_buf, sem, step, slot, steps_per_seq):
    g_pages = PAGES_PER_STEP
    seq = lax.div(step, steps_per_seq)
    first = lax.rem(step, steps_per_seq) * g_pages
    copies = []
    for g in range(g_pages):
        page = pt_ref[seq, first + g]
        copies.append(pltpu.make_async_copy(lat_hbm.at[0, page], lat_buf.at[slot, g], sem.at[slot, 0]))
        copies.append(pltpu.make_async_copy(rope_hbm.at[0, page], rope_buf.at[slot, g], sem.at[slot, 1]))
    return copies


def _decode_kernel(pt_ref, qlat_ref, qrope_ref, latn_ref, ropen_ref, lat_hbm, rope_hbm, o_ref,
                   lat_buf, rope_buf, sem, m_sc, l_sc, acc_sc, *, n_seq, nj):
    g_pages = PAGES_PER_STEP
    j = pl.program_id(1)
    step = pl.program_id(0) * nj + j
    n_steps = n_seq * nj
    slot = lax.rem(step, 2)
    copies = functools.partial(_page_copies, pt_ref, lat_hbm, rope_hbm, lat_buf, rope_buf, sem,
                               steps_per_seq=nj)

    @pl.when(step == 0)
    def _():
        for c in copies(step=step, slot=slot):
            c.start()

    @pl.when(step + 1 < n_steps)
    def _():
        for c in copies(step=step + 1, slot=1 - slot):
            c.start()

    for c in copies(step=step, slot=slot):
        c.wait()

    @pl.when(j == 0)
    def _():
        m_sc[...] = jnp.full(m_sc.shape, NEG_BIG, F32)
        l_sc[...] = jnp.zeros(l_sc.shape, F32)
        acc_sc[...] = jnp.zeros(acc_sc.shape, F32)

    ql = qlat_ref[0]
    qr = qrope_ref[0]
    per_chain = g_pages // DECODE_CHAINS
    chains = []
    for c in range(DECODE_CHAINS):
        lats = []
        ss = []
        for g in range(c * per_chain, (c + 1) * per_chain):
            lat = lat_buf[slot, g].astype(BF16)
            rpt = rope_buf[slot, g].astype(BF16)
            lats.append(lat)
            ss.append(_dot_nt(ql, lat) + _dot(qr, rpt))
        chains.append((jnp.concatenate(ss, axis=1), lats))
    m_run = m_sc[...]
    l_run = l_sc[...]
    acc = acc_sc[...]
    for s, lats in chains:
        m_next = jnp.maximum(m_run, jnp.max(s, axis=1, keepdims=True))
        alpha = jnp.exp2(m_run - m_next)
        p = jnp.exp2(s - m_next[:, 0:1])
        l_run = alpha * l_run + jnp.sum(p, axis=1, keepdims=True)
        pb = p.astype(BF16)
        pv = _dot(pb[:, 0:PAGE_SIZE], lats[0])
        for g in range(1, per_chain):
            pv = pv + _dot(pb[:, g * PAGE_SIZE:(g + 1) * PAGE_SIZE], lats[g])
        acc = acc * alpha[:, 0:1] + pv
        m_run = m_next
    m_sc[...] = m_run
    l_sc[...] = l_run
    acc_sc[...] = acc

    @pl.when(j == nj - 1)
    def _():
        qlf = ql.astype(F32)
        qrf = qr.astype(F32)
        s_new = (jnp.sum(qlf * latn_ref[0], axis=1, keepdims=True)
                 + jnp.sum(qrf * ropen_ref[0], axis=1, keepdims=True))
        m_old = m_sc[...][:, 0:1]
        m_fin = jnp.maximum(m_old, s_new)
        a_old = jnp.exp2(m_old - m_fin)
        p_new = jnp.exp2(s_new - m_fin)
        l_fin = a_old * l_sc[...][:, 0:1] + p_new
        o_ref[0] = (acc_sc[...] * a_old + p_new * latn_ref[0]) / l_fin


def _decode(page_table, qlat, qrope, lat_new, rope_new, cache_lat, cache_rope):
    b = qlat.shape[0]
    n_pages = page_table.shape[1]
    g_pages = PAGES_PER_STEP
    in_specs = [
        pl.BlockSpec((1, N_HEADS, KV_RANK), lambda s, j, pt: (s, 0, 0)),
        pl.BlockSpec((1, N_HEADS, QK_ROPE), lambda s, j, pt: (s, 0, 0)),
        pl.BlockSpec((1, 1, KV_RANK), lambda s, j, pt: (s, 0, 0)),
        pl.BlockSpec((1, 1, QK_ROPE), lambda s, j, pt: (s, 0, 0)),
        pl.BlockSpec(memory_space=pl.ANY),
        pl.BlockSpec(memory_space=pl.ANY),
    ]
    assert n_pages % g_pages == 0
    grid_spec = pltpu.PrefetchScalarGridSpec(
        num_scalar_prefetch=1,
        grid=(b, n_pages // g_pages),
        in_specs=in_specs,
        out_specs=pl.BlockSpec((1, N_HEADS, KV_RANK), lambda s, j, pt: (s, 0, 0)),
        scratch_shapes=[
            pltpu.VMEM((2, g_pages, PAGE_SIZE, KV_RANK), F32),
            pltpu.VMEM((2, g_pages, QK_ROPE, PAGE_SIZE), F32),
            pltpu.SemaphoreType.DMA((2, 2)),
            pltpu.VMEM((N_HEADS, LANES), F32),
            pltpu.VMEM((N_HEADS, LANES), F32),
            pltpu.VMEM((N_HEADS, KV_RANK), F32),
        ],
    )
    return pl.pallas_call(
        functools.partial(_decode_kernel, n_seq=b, nj=n_pages // g_pages), grid_spec=grid_spec,
        out_shape=jax.ShapeDtypeStruct((b, N_HEADS, KV_RANK), F32),
        compiler_params=pltpu.CompilerParams(dimension_semantics=("arbitrary", "arbitrary"),
                                             vmem_limit_bytes=VMEM_LIMIT),
        name="sample_decode",
    )(page_table, qlat, qrope, lat_new, rope_new, cache_lat, cache_rope)


def _spost_kernel(x_ref, olat_ref, cvn_ref, mod_ref, wuvbd_ref, ona_ref, wout_ref, norm2_ref, wup_ref,
                  fcw_ref, fcb_ref, wdn_ref, fn_ref, st_ref, y_out, st_out):
    mod = mod_ref[...]
    g1 = mod[:, 2 * D_MODEL:3 * D_MODEL]
    sh2 = mod[:, 3 * D_MODEL:4 * D_MODEL]
    sc2 = mod[:, 4 * D_MODEL:5 * D_MODEL]
    g2 = mod[:, 5 * D_MODEL:6 * D_MODEL]
    attn = _dot(olat_ref[...].astype(BF16), wuvbd_ref[...])
    an = _rms(attn, ona_ref[...])
    x1 = _mix_residual(x_ref[...], an.astype(BF16), cvn_ref[...], g1, wout_ref)
    h2 = (_rms(x1, norm2_ref[...]) * (1.0 + sc2) + sh2).astype(BF16)
    u = _dot(h2, wup_ref[...])
    w = 2 * D_FF
    up = fcb_ref[...] + fcw_ref[0:1, :] * st_ref[:, 0:w] + fcw_ref[1:2, :] * st_ref[:, w:2 * w] + fcw_ref[2:3, :] * u
    st_out[:, 0:w] = st_ref[:, w:2 * w]
    st_out[:, w:2 * w] = u
    act = (_silu(up[:, 0:D_FF]) * up[:, D_FF:w]).astype(BF16)
    x2 = x1 + g2 * _dot(act, wdn_ref[...])
    y_out[...] = _rms(x2, fn_ref[...])


def _sample_post(x, olat, cvn, mod_s, wuvbd, ona, wout, norm2, wup, fcw, fcb, wdn, fnorm, st):
    b = x.shape[0]
    out_shape = (jax.ShapeDtypeStruct((b, D_MODEL), F32), jax.ShapeDtypeStruct(st.shape, F32))
    return pl.pallas_call(
        _spost_kernel, out_shape=out_shape,
        compiler_params=pltpu.CompilerParams(vmem_limit_bytes=VMEM_LIMIT),
        name="sample_post",
    )(x, olat, cvn, mod_s, wuvbd, ona, wout, norm2, wup, fcw, fcb, wdn, fnorm, st)


def _rope_tables(pos):
    freqs = ROPE_THETA ** (-jnp.arange(ROPE_HALF, dtype=F32) / ROPE_HALF)
    ang = pos.astype(F32)[:, None] * freqs[None, :]
    cos = jnp.cos(ang)
    sin = jnp.sin(ang)
    s = pos.shape[0]
    c2 = jnp.concatenate([cos, cos], axis=1)
    s2 = jnp.concatenate([sin, sin], axis=1)
    zpad = jnp.zeros((s, HEAD_PAD - QK_ROPE), F32)
    qscale = ATTN_SCALE * LOG2E
    cosq = jnp.concatenate([c2, jnp.ones((s, QK_NOPE), F32), jnp.zeros((s, HEAD_PAD - QK_ROPE - QK_NOPE), F32)],
                           axis=1) * qscale
    sinq = jnp.concatenate([s2, zpad], axis=1) * qscale
    cosk = jnp.concatenate([c2, zpad], axis=1)
    sink = jnp.concatenate([s2, zpad], axis=1)
    return jnp.stack([cosq, sinq, cosk, sink], axis=0)


def _rot_half_cols(w):
    return jnp.concatenate([-w[..., ROPE_HALF:], w[..., :ROPE_HALF]], axis=-1)


def _pack_weights(w_in, w_uq, w_uk, w_uv):
    o1, o2, o3 = Q_RANK, Q_RANK + KV_RANK, Q_RANK + KV_RANK + QK_ROPE
    wkr = w_in[:, o2:o3]
    zpad = jnp.zeros((D_MODEL, HEAD_PAD - QK_ROPE), F32)
    wz = jnp.concatenate([w_in[:, :o2], w_in[:, o3:], wkr, zpad, _rot_half_cols(wkr), zpad], axis=1).astype(BF16)

    wq = w_uq.reshape(Q_RANK, N_HEADS, QK_NOPE + QK_ROPE)
    wq_nope, wq_rope = wq[..., :QK_NOPE], wq[..., QK_NOPE:]
    ztail = jnp.zeros((Q_RANK, N_HEADS, HEAD_PAD - QK_ROPE - QK_NOPE), F32)
    wq_plain = jnp.concatenate([wq_rope, wq_nope, ztail], axis=2).reshape(Q_RANK, N_HEADS * HEAD_PAD)
    wq_rot = jnp.concatenate([_rot_half_cols(wq_rope), jnp.zeros((Q_RANK, N_HEADS, HEAD_PAD - QK_ROPE), F32)],
                             axis=2).reshape(Q_RANK, N_HEADS * HEAD_PAD)
    wq2 = jnp.concatenate([wq_plain, wq_rot], axis=1).astype(BF16)

    wuk = jnp.concatenate([jnp.zeros((KV_RANK, N_HEADS, QK_ROPE), F32), w_uk,
                           jnp.zeros((KV_RANK, N_HEADS, HEAD_PAD - QK_ROPE - QK_NOPE), F32)],
                          axis=2).reshape(KV_RANK, N_HEADS * HEAD_PAD).astype(BF16)
    wuv = w_uv.reshape(KV_RANK, ATTN_WIDTH).astype(BF16)

    wukt = jnp.transpose(w_uk, (1, 2, 0))
    wukt = jnp.concatenate([jnp.zeros((N_HEADS, QK_ROPE, KV_RANK), F32), wukt,
                            jnp.zeros((N_HEADS, HEAD_PAD - QK_ROPE - QK_NOPE, KV_RANK), F32)], axis=1)
    eye = jnp.eye(N_HEADS, dtype=F32)
    wukt_bd = (eye[:, None, :, None] * wukt[:, :, None, :]).reshape(N_HEADS * HEAD_PAD, N_HEADS * KV_RANK).astype(BF16)
    wuvh = jnp.transpose(w_uv, (1, 0, 2))
    wuv_bd = (eye[:, None, :, None] * wuvh[:, :, None, :]).reshape(N_HEADS * KV_RANK, ATTN_WIDTH).astype(BF16)
    return wz, wq2, wuk, wuv, wukt_bd, wuv_bd


def kernel(x_prompt, x_sample, c_prompt, c_sample, cache_kv_latent, cache_k_rope, state_conv, state_ffn_conv, page_table, w_ada, b_ada, norm1, w_in, q_norm, w_uq, kv_norm, w_uk, w_uv, conv_w, conv_b, conv_ln_g, conv_ln_b, out_norm_attn, out_norm_conv, w_out, norm2, w_up, ffn_conv_w, ffn_conv_b, w_down, final_norm):
    assert w_ada.shape[0] == 1, "single-layer decoder"
    n_p, s_p, _ = x_prompt.shape
    n_s, s_s, _ = x_sample.shape
    assert s_s == 1

    row = lambda a: a.reshape(1, -1)
    wz, wq2, wuk, wuv, wukt_bd, wuv_bd = _pack_weights(w_in[0], w_uq[0], w_uk[0], w_uv[0])
    wout = w_out[0].astype(BF16)
    wup = w_up[0].astype(BF16)
    wdn = w_down[0].astype(BF16)
    norm1_r, qn_r, kvn_r = row(norm1[0]), row(q_norm[0]), row(kv_norm[0])
    cw, cb = conv_w[0], row(conv_b[0])
    lng, lnb, onc, ona = row(conv_ln_g[0]), row(conv_ln_b[0]), row(out_norm_conv[0]), row(out_norm_attn[0])
    norm2_r, fcw, fcb, fnorm = row(norm2[0]), ffn_conv_w[0], row(ffn_conv_b[0]), row(final_norm)

    n_c = n_p + n_s
    n_c_pad = -(-n_c // SUBLANES) * SUBLANES
    c_all = jnp.concatenate([c_prompt, c_sample, jnp.zeros((n_c_pad - n_c, D_MODEL), F32)], axis=0)
    mod = _modulation(c_all, w_ada[0], b_ada[0])
    mod_p = mod[:n_p].reshape(n_p, N_MOD, D_MODEL)
    mod_s = mod[n_p:n_c]

    tab_p = _rope_tables(jnp.arange(s_p))
    tabq_t = jnp.transpose(tab_p[0:2], (0, 2, 1))
    qt, k, vt, kv_lat_p, k_rope_p, cvn_p, conv_st_p = _prompt_pre(
        x_prompt, mod_p, tabq_t, tab_p[2:4], norm1_r, wz, qn_r, wq2.T, kvn_r, wuk, wuv.T, cw, cb, lng, lnb, onc)
    attn_p = _flash(qt, k, vt)
    y_p, ffn_st_p = _prompt_post(x_prompt, attn_p, cvn_p, mod_p, ona, wout, norm2_r, wup, fcw, fcb, wdn, fnorm)

    tab_s = _rope_tables(PAST_LEN + jnp.arange(s_s))
    st_conv = state_conv[0].reshape(n_s, (CONV_K - 1) * CONV_WIDTH)
    qlat, q_s, lat_s, kr_s, cvn_s, st_conv_new = _sample_pre(
        x_sample[:, 0], mod_s, tab_s, norm1_r, wz, qn_r, wq2, kvn_r, wukt_bd, cw, cb, lng, lnb, onc, st_conv)
    qrope = q_s.reshape(n_s, N_HEADS, HEAD_PAD)[:, :, :QK_ROPE]
    o_lat = _decode(page_table, qlat.reshape(n_s, N_HEADS, KV_RANK), qrope,
                    lat_s.reshape(n_s, 1, KV_RANK), kr_s.reshape(n_s, 1, QK_ROPE),
                    cache_kv_latent, jnp.swapaxes(cache_k_rope, 2, 3))
    st_ffn = state_ffn_conv[0].reshape(n_s, (FFN_K - 1) * 2 * D_FF)
    y_s, st_ffn_new = _sample_post(
        x_sample[:, 0], o_lat.reshape(n_s, N_HEADS * KV_RANK), cvn_s, mod_s, wuv_bd, ona, wout, norm2_r,
        wup, fcw, fcb, wdn, fnorm, st_ffn)

    return (
        y_p,
        y_s.reshape(n_s, 1, D_MODEL),
        kv_lat_p,
        k_rope_p,
        conv_st_p,
        ffn_st_p,
        lat_s.reshape(1, n_s, 1, KV_RANK),
        kr_s.reshape(1, n_s, 1, QK_ROPE),
        st_conv_new.reshape(1, n_s, CONV_K - 1, CONV_WIDTH),
        st_ffn_new.reshape(1, n_s, FFN_K - 1, 2 * D_FF),
    )
```

```python
import functools
import math

import jax
import jax.numpy as jnp
from jax import lax
from jax.experimental import pallas as pl
from jax.experimental.pallas import tpu as pltpu

D_MODEL = 1024
N_HEADS = 8
QK_NOPE = 64
QK_ROPE = 32
V_DIM = 64
Q_RANK = 384
KV_RANK = 256
ATTN_WIDTH = N_HEADS * V_DIM
CONV_WIDTH = 512
CONV_K = 31
D_FF = 2816
FFN_K = 3
N_MOD = 6
ROPE_THETA = 10000.0
EPS = 1e-6
PAST_LEN = 16384
PAGE_SIZE = 128
ATTN_SCALE = (QK_NOPE + QK_ROPE) ** -0.5
LOG2E = math.log2(math.e)

LANES = 128
SUBLANES = 8
HEAD_PAD = LANES
V_EXT = V_DIM + 16
ROPE_HALF = QK_ROPE // 2
NEG_BIG = -1e30

TM_PRE = 512
TM_POST = 512
FF_CHUNK = 1408
TQ = 2048
TK = 512
FLASH_LOOKAHEAD = 2
PAGES_PER_STEP = 64
DECODE_CHAINS = 2
CARRY_ROWS = 32
VMEM_LIMIT = 56 * 1024 * 1024

Z_CQ = 0
Z_CKV = Z_CQ + Q_RANK
Z_GA = Z_CKV + KV_RANK
Z_GB = Z_GA + CONV_WIDTH
Z_KR = Z_GB + CONV_WIDTH
Z_KRR = Z_KR + HEAD_PAD
Z_END = Z_KRR + HEAD_PAD

BF16 = jnp.bfloat16
F32 = jnp.float32


def _dot(a, b):
    return jnp.dot(a, b, preferred_element_type=F32)


def _dot_nt(a, b):
    return lax.dot_general(a, b, (((1,), (1,)), ((), ())), preferred_element_type=F32)


def _rms(x, g):
    return x * lax.rsqrt(jnp.mean(x * x, axis=-1, keepdims=True) + EPS) * g


def _silu(x):
    return x * jax.nn.sigmoid(x)


def _mod_kernel(c_ref, w_ref, b_ref, o_ref):
    s = _silu(c_ref[...])
    o_ref[...] = _dot(s.astype(BF16), w_ref[...].astype(BF16)) + b_ref[...]


def _modulation(c, w_ada, b_ada):
    n = c.shape[0]
    tn = 1024
    return pl.pallas_call(
        _mod_kernel,
        grid=(N_MOD * D_MODEL // tn,),
        in_specs=[
            pl.BlockSpec((n, D_MODEL), lambda j: (0, 0)),
            pl.BlockSpec((D_MODEL, tn), lambda j: (0, j)),
            pl.BlockSpec((1, tn), lambda j: (0, j)),
        ],
        out_specs=pl.BlockSpec((n, tn), lambda j: (0, j)),
        out_shape=jax.ShapeDtypeStruct((n, N_MOD * D_MODEL), F32),
        compiler_params=pltpu.CompilerParams(dimension_semantics=("arbitrary",), vmem_limit_bytes=VMEM_LIMIT),
        name="adaln_mod",
    )(c, w_ada, b_ada.reshape(1, -1))


def _front(x, sh1, sc1, norm1, wz_ref, qn, kvn, cosk, sink):
    h = _rms(x, norm1) * (1.0 + sc1) + sh1
    z = _dot(h.astype(BF16), wz_ref[...])
    cqn = _rms(z[:, Z_CQ:Z_CKV], qn).astype(BF16)
    lat = _rms(z[:, Z_CKV:Z_GA], kvn)
    kro = z[:, Z_KR:Z_KRR] * cosk + z[:, Z_KRR:Z_END] * sink
    glu = z[:, Z_GA:Z_GB] * jax.nn.sigmoid(z[:, Z_GB:Z_KR])
    return cqn, lat, kro, glu


def _conv_tail(cv, lng, lnb, onc):
    mu = jnp.mean(cv, axis=-1, keepdims=True)
    d = cv - mu
    y = d * lax.rsqrt(jnp.mean(d * d, axis=-1, keepdims=True) + EPS) * lng + lnb
    return _rms(_silu(y), onc)


def _pre_kernel(x_ref, mod_ref, tabq_ref, tabk_ref, norm1_ref, wz_ref, qn_ref, wq2t_ref, kvn_ref, wuk_ref, wuvt_ref,
                cw_ref, cb_ref, lng_ref, lnb_ref, onc_ref,
                qt_out, k_out, vt_out, lat_out, kr_out, cvn_out, cst_out,
                ext_ref, shift_ref):
    i = pl.program_id(1)
    tm = x_ref.shape[1]
    m = mod_ref[0]
    cqn, lat, kro, glu = _front(x_ref[0], m[0:1], m[1:2], norm1_ref[...], wz_ref, qn_ref[...],
                                kvn_ref[...], tabk_ref[0], tabk_ref[1])
    q2t = _dot_nt(wq2t_ref[...], cqn)
    cosqt = tabq_ref[0]
    sinqt = tabq_ref[1]
    hw = N_HEADS * HEAD_PAD
    for hd in range(N_HEADS):
        lo = hd * HEAD_PAD
        qt_out[0, 0, lo:lo + HEAD_PAD, :] = (q2t[lo:lo + HEAD_PAD, :] * cosqt
                                             + q2t[hw + lo:hw + lo + HEAD_PAD, :] * sinqt).astype(BF16)
    lat_out[0, 0] = lat
    kr_out[0, 0] = kro[:, :QK_ROPE]
    latb = lat.astype(BF16)
    kn = _dot(latb, wuk_ref[...])
    for hd in range(N_HEADS):
        k_out[0, :, hd * HEAD_PAD:(hd + 1) * HEAD_PAD] = (kn[:, hd * HEAD_PAD:(hd + 1) * HEAD_PAD] + kro).astype(BF16)
    vt = _dot_nt(wuvt_ref[...], latb)
    vrow = lax.broadcasted_iota(jnp.int32, vt.shape, 0)
    is_one = functools.reduce(jnp.logical_or, [vrow == hd * V_EXT + V_DIM for hd in range(N_HEADS)])
    vt_out[0, 0] = jnp.where(is_one, 1.0, vt).astype(BF16)

    @pl.when(i == 0)
    def _():
        ext_ref[0:CARRY_ROWS, :] = jnp.zeros((CARRY_ROWS, CONV_WIDTH), F32)

    ext_ref[CARRY_ROWS:CARRY_ROWS + tm, :] = glu
    base = CARRY_ROWS - (CONV_K - 1)
    cv = jnp.broadcast_to(cb_ref[...], (tm, CONV_WIDTH))
    for r in range(SUBLANES):
        taps = [k for k in range(CONV_K) if (base + k) % SUBLANES == r]
        if not taps:
            continue
        span = max((base + k) // SUBLANES for k in taps) * SUBLANES
        if r:
            shift_ref[0:tm + span, :] = ext_ref[pl.ds(r, tm + span), :]
        src = shift_ref if r else ext_ref
        for k in taps:
            a = (base + k) // SUBLANES * SUBLANES
            cv = cv + cw_ref[k:k + 1, :] * src[pl.ds(a, tm), :]
    ext_ref[0:CARRY_ROWS, :] = ext_ref[tm:tm + CARRY_ROWS, :]

    @pl.when(i == pl.num_programs(1) - 1)
    def _():
        cst_out[0, 0] = ext_ref[base:CARRY_ROWS, :]

    cvn_out[0] = _conv_tail(cv, lng_ref[...], lnb_ref[...], onc_ref[...]).astype(BF16)


def _const_spec(shape):
    nd = len(shape)
    return pl.BlockSpec(shape, lambda *_: (0,) * nd, pipeline_mode=pl.Buffered(1))


def _prompt_pre(x, mod_p, tabq, tabk, norm1, wz, qn, wq2t, kvn, wuk, wuvt, cw, cb, lng, lnb, onc):
    n, s, _ = x.shape
    tm = TM_PRE
    grid = (n, s // tm)
    tok = lambda w: pl.BlockSpec((1, tm, w), lambda b, i: (b, i, 0))
    tok_t = lambda w: pl.BlockSpec((1, 1, w, tm), lambda b, i: (b, i, 0, 0))
    in_specs = [
        tok(D_MODEL),
        pl.BlockSpec((1, N_MOD, D_MODEL), lambda b, i: (b, 0, 0)),
        pl.BlockSpec((2, HEAD_PAD, tm), lambda b, i: (0, 0, i)),
        pl.BlockSpec((2, tm, LANES), lambda b, i: (0, i, 0)),
        _const_spec(norm1.shape), _const_spec(wz.shape), _const_spec(qn.shape), _const_spec(wq2t.shape),
        _const_spec(kvn.shape), _const_spec(wuk.shape), _const_spec(wuvt.shape),
        _const_spec(cw.shape), _const_spec(cb.shape), _const_spec(lng.shape), _const_spec(lnb.shape),
        _const_spec(onc.shape),
    ]
    out_shape = (
        jax.ShapeDtypeStruct((n, s // tm, N_HEADS * HEAD_PAD, tm), BF16),
        jax.ShapeDtypeStruct((n, s, N_HEADS * HEAD_PAD), BF16),
        jax.ShapeDtypeStruct((n, s // tm, N_HEADS * V_EXT, tm), BF16),
        jax.ShapeDtypeStruct((1, n, s, KV_RANK), F32),
        jax.ShapeDtypeStruct((1, n, s, QK_ROPE), F32),
        jax.ShapeDtypeStruct((n, s, CONV_WIDTH), BF16),
        jax.ShapeDtypeStruct((1, n, CONV_K - 1, CONV_WIDTH), F32),
    )
    out_specs = (
        tok_t(N_HEADS * HEAD_PAD), tok(N_HEADS * HEAD_PAD), tok_t(N_HEADS * V_EXT),
        pl.BlockSpec((1, 1, tm, KV_RANK), lambda b, i: (0, b, i, 0)),
        pl.BlockSpec((1, 1, tm, QK_ROPE), lambda b, i: (0, b, i, 0)),
        tok(CONV_WIDTH),
        pl.BlockSpec((1, 1, CONV_K - 1, CONV_WIDTH), lambda b, i: (0, b, 0, 0)),
    )
    return pl.pallas_call(
        _pre_kernel, grid=grid, in_specs=in_specs, out_specs=out_specs, out_shape=out_shape,
        scratch_shapes=[pltpu.VMEM((CARRY_ROWS + tm, CONV_WIDTH), F32),
                        pltpu.VMEM((CARRY_ROWS + tm, CONV_WIDTH), F32)],
        compiler_params=pltpu.CompilerParams(dimension_semantics=("arbitrary", "arbitrary"),
                                             vmem_limit_bytes=VMEM_LIMIT),
        name="prompt_pre",
    )(x, mod_p, tabq, tabk, norm1, wz, qn, wq2t, kvn, wuk, wuvt, cw, cb, lng, lnb, onc)


def _flash_kernel(qi_ref, ki_ref, qt_ref, k_ref, vt_ref, o_ref, m_sc, acc_sc):
    t = pl.program_id(1)
    qi = qi_ref[t]
    ki = ki_ref[t]
    n_groups, wq = qt_ref.shape[1], qt_ref.shape[3]
    tq = n_groups * wq
    tk = k_ref.shape[1]

    @pl.when(ki == 0)
    def _():
        m_sc[...] = jnp.full(m_sc.shape, NEG_BIG, F32)
        acc_sc[...] = jnp.zeros(acc_sc.shape, F32)

    units = [(hd, c * wq) for hd in range(N_HEADS) for c in range(n_groups)]

    def scores(unit):
        hd, q0 = unit
        qt = qt_ref[0, q0 // wq, hd * HEAD_PAD:(hd + 1) * HEAD_PAD, :]
        k = k_ref[0, :, hd * HEAD_PAD:(hd + 1) * HEAD_PAD]
        return _dot(k, qt)

    def step(key0):
        live = units
        if key0 is not None:
            key = lax.broadcasted_iota(jnp.int32, (tk, wq), 0)
            qry = lax.broadcasted_iota(jnp.int32, (tk, wq), 1)
            live = [(hd, q0) for hd, q0 in units if q0 + wq > key0]
        pending = [scores(u) for u in live[:FLASH_LOOKAHEAD]]
        for i, (hd, q0) in enumerate(live):
            st = pending.pop(0)
            if i + FLASH_LOOKAHEAD < len(live):
                pending.append(scores(live[i + FLASH_LOOKAHEAD]))
            if key0 is not None and q0 < key0 + tk - 1:
                st = jnp.where(key + key0 <= qry + q0, st, NEG_BIG)
            vt = vt_ref[0, 0, hd * V_EXT:(hd + 1) * V_EXT, :]
            m_prev = m_sc[hd, :, q0:q0 + wq]
            m_next = jnp.maximum(m_prev, jnp.max(st, axis=0, keepdims=True))
            alpha = jnp.exp2(m_prev - m_next)
            p = jnp.exp2(st - m_next)
            acc_sc[hd, :, q0:q0 + wq] = acc_sc[hd, :, q0:q0 + wq] * alpha + _dot(vt, p.astype(BF16))
            m_sc[hd, :, q0:q0 + wq] = m_next

    ratio = tq // tk

    @pl.when(ki < qi * ratio)
    def _():
        step(None)

    for d in range(ratio):
        @pl.when(ki == qi * ratio + d)
        def _(d=d):
            step(d * tk)

    @pl.when(ki == qi * ratio + ratio - 1)
    def _():
        for pr in range(N_HEADS // 2):
            halves = []
            for hd in (2 * pr, 2 * pr + 1):
                inv = 1.0 / acc_sc[hd, V_DIM:V_DIM + 1, :]
                halves.append(acc_sc[hd, 0:V_DIM, :] * inv)
            o_ref[0, :, pr * LANES:(pr + 1) * LANES] = jnp.concatenate(halves, axis=0).T


def _flash(qt, k, vt):
    n, s, _ = k.shape
    tq, tk = TQ, TK
    tile = qt.shape[3]
    assert tq % tk == 0 and s % tq == 0 and tq % tile == 0 and tk == tile
    pairs = [(a, b) for a in range(s // tq) for b in range((a + 1) * (tq // tk))]
    qi = jnp.asarray([p[0] for p in pairs], jnp.int32)
    ki = jnp.asarray([p[1] for p in pairs], jnp.int32)
    grid_spec = pltpu.PrefetchScalarGridSpec(
        num_scalar_prefetch=2,
        grid=(n, len(pairs)),
        in_specs=[
            pl.BlockSpec((1, tq // tile, N_HEADS * HEAD_PAD, tile), lambda b, t, qi, ki: (b, qi[t], 0, 0)),
            pl.BlockSpec((1, tk, N_HEADS * HEAD_PAD), lambda b, t, qi, ki: (b, ki[t], 0)),
            pl.BlockSpec((1, 1, N_HEADS * V_EXT, tk), lambda b, t, qi, ki: (b, ki[t], 0, 0)),
        ],
        out_specs=pl.BlockSpec((1, tq, ATTN_WIDTH), lambda b, t, qi, ki: (b, qi[t], 0)),
        scratch_shapes=[
            pltpu.VMEM((N_HEADS, 1, tq), F32),
            pltpu.VMEM((N_HEADS, V_EXT, tq), F32),
        ],
    )
    return pl.pallas_call(
        _flash_kernel, grid_spec=grid_spec,
        out_shape=jax.ShapeDtypeStruct((n, s, ATTN_WIDTH), F32),
        compiler_params=pltpu.CompilerParams(dimension_semantics=("arbitrary", "arbitrary"),
                                             vmem_limit_bytes=VMEM_LIMIT),
        name="prompt_flash",
    )(qi, ki, qt, k, vt)


def _mix_residual(x, an_b, cvn_b, g1, wout_ref):
    mix = _dot(an_b, wout_ref[0:ATTN_WIDTH, :]) + _dot(cvn_b, wout_ref[ATTN_WIDTH:, :])
    return x + g1 * mix


def _post_kernel(x_ref, attn_ref, cvn_ref, mod_ref, ona_ref, wout_ref, norm2_ref, wup_ref, fcw_ref, fcb_ref,
                 wdn_ref, fn_ref, y_out, fst_out, ext_ref, carry_ref):
    i = pl.program_id(1)
    tm = x_ref.shape[1]
    m = mod_ref[0]
    g1, sh2, sc2, g2 = m[2:3], m[3:4], m[4:5], m[5:6]
    an = _rms(attn_ref[0], ona_ref[...])
    x1 = _mix_residual(x_ref[0], an.astype(BF16), cvn_ref[0], g1, wout_ref)
    h2 = (_rms(x1, norm2_ref[...]) * (1.0 + sc2) + sh2).astype(BF16)

    @pl.when(i == 0)
    def _():
        carry_ref[...] = jnp.zeros(carry_ref.shape, F32)

    def conv_part(off):
        u = _dot(h2, wup_ref[:, off:off + FF_CHUNK])
        ext_ref[0:SUBLANES, :] = carry_ref[:, off:off + FF_CHUNK]
        ext_ref[SUBLANES:SUBLANES + tm, :] = u
        carry_ref[:, off:off + FF_CHUNK] = ext_ref[tm:tm + SUBLANES, :]
        up = fcb_ref[:, off:off + FF_CHUNK] + fcw_ref[2:3, off:off + FF_CHUNK] * u
        for k in range(FFN_K - 1):
            up = up + fcw_ref[k:k + 1, off:off + FF_CHUNK] * ext_ref[pl.ds(SUBLANES - (FFN_K - 1) + k, tm), :]
        return up

    ffn = jnp.zeros((tm, D_MODEL), F32)
    for c in range(D_FF // FF_CHUNK):
        a = conv_part(c * FF_CHUNK)
        v = conv_part(D_FF + c * FF_CHUNK)
        act = (_silu(a) * v).astype(BF16)
        ffn = ffn + _dot(act, wdn_ref[c * FF_CHUNK:(c + 1) * FF_CHUNK, :])
    x2 = x1 + g2 * ffn
    y_out[0] = _rms(x2, fn_ref[...])

    @pl.when(i == pl.num_programs(1) - 1)
    def _():
        fst_out[0, 0] = carry_ref[SUBLANES - (FFN_K - 1):SUBLANES, :]


def _prompt_post(x, attn, cvn, mod_p, ona, wout, norm2, wup, fcw, fcb, wdn, fnorm):
    n, s, _ = x.shape
    tm = TM_POST
    tok = lambda w: pl.BlockSpec((1, tm, w), lambda b, i: (b, i, 0))
    in_specs = [
        tok(D_MODEL), tok(ATTN_WIDTH), tok(CONV_WIDTH),
        pl.BlockSpec((1, N_MOD, D_MODEL), lambda b, i: (b, 0, 0)),
        _const_spec(ona.shape), _const_spec(wout.shape), _const_spec(norm2.shape), _const_spec(wup.shape),
        _const_spec(fcw.shape), _const_spec(fcb.shape), _const_spec(wdn.shape), _const_spec(fnorm.shape),
    ]
    out_shape = (
        jax.ShapeDtypeStruct((n, s, D_MODEL), F32),
        jax.ShapeDtypeStruct((1, n, FFN_K - 1, 2 * D_FF), F32),
    )
    out_specs = (
        tok(D_MODEL),
        pl.BlockSpec((1, 1, FFN_K - 1, 2 * D_FF), lambda b, i: (0, b, 0, 0)),
    )
    return pl.pallas_call(
        _post_kernel, grid=(n, s // tm), in_specs=in_specs, out_specs=out_specs, out_shape=out_shape,
        scratch_shapes=[pltpu.VMEM((SUBLANES + tm, FF_CHUNK), F32), pltpu.VMEM((SUBLANES, 2 * D_FF), F32)],
        compiler_params=pltpu.CompilerParams(dimension_semantics=("arbitrary", "arbitrary"),
                                             vmem_limit_bytes=VMEM_LIMIT),
        name="prompt_post",
    )(x, attn, cvn, mod_p, ona, wout, norm2, wup, fcw, fcb, wdn, fnorm)


def _spre_kernel(x_ref, mod_ref, tab_ref, norm1_ref, wz_ref, qn_ref, wq2_ref, kvn_ref, wukt_ref,
                 cw_ref, cb_ref, lng_ref, lnb_ref, onc_ref, st_ref,
                 qlat_out, q_out, lat_out, kr_out, cvn_out, st_out):
    mod = mod_ref[...]
    sh1 = mod[:, 0:D_MODEL]
    sc1 = mod[:, D_MODEL:2 * D_MODEL]
    cqn, lat, kro, glu = _front(x_ref[...], sh1, sc1, norm1_ref[...], wz_ref, qn_ref[...],
                                kvn_ref[...], tab_ref[2], tab_ref[3])
    q2 = _dot(cqn, wq2_ref[...])
    qs = []
    for hd in range(N_HEADS):
        a = q2[:, hd * HEAD_PAD:(hd + 1) * HEAD_PAD]
        b = q2[:, (N_HEADS + hd) * HEAD_PAD:(N_HEADS + hd + 1) * HEAD_PAD]
        qs.append(a * tab_ref[0] + b * tab_ref[1])
    q = jnp.concatenate(qs, axis=1).astype(BF16)
    q_out[...] = q
    qlat_out[...] = _dot(q, wukt_ref[...]).astype(BF16)
    lat_out[...] = lat
    kr_out[...] = kro[:, :QK_ROPE]
    kst = CONV_K - 1
    cv = cb_ref[...] + cw_ref[kst:kst + 1, :] * glu
    for k in range(kst):
        cv = cv + cw_ref[k:k + 1, :] * st_ref[:, k * CONV_WIDTH:(k + 1) * CONV_WIDTH]
    st_out[:, 0:(kst - 1) * CONV_WIDTH] = st_ref[:, CONV_WIDTH:kst * CONV_WIDTH]
    st_out[:, (kst - 1) * CONV_WIDTH:kst * CONV_WIDTH] = glu
    cvn_out[...] = _conv_tail(cv, lng_ref[...], lnb_ref[...], onc_ref[...]).astype(BF16)


def _sample_pre(x, mod_s, tab, norm1, wz, qn, wq2, kvn, wukt, cw, cb, lng, lnb, onc, st):
    b = x.shape[0]
    out_shape = (
        jax.ShapeDtypeStruct((b, N_HEADS * KV_RANK), BF16),
        jax.ShapeDtypeStruct((b, N_HEADS * HEAD_PAD), BF16),
        jax.ShapeDtypeStruct((b, KV_RANK), F32),
        jax.ShapeDtypeStruct((b, QK_ROPE), F32),
        jax.ShapeDtypeStruct((b, CONV_WIDTH), BF16),
        jax.ShapeDtypeStruct(st.shape, F32),
    )
    return pl.pallas_call(
        _spre_kernel, out_shape=out_shape,
        compiler_params=pltpu.CompilerParams(vmem_limit_bytes=VMEM_LIMIT),
        name="sample_pre",
    )(x, mod_s, tab, norm1, wz, qn, wq2, kvn, wukt, cw, cb, lng, lnb, onc, st)


def _page_copies(pt_ref, lat_hbm, rope_hbm, lat_buf, rope_buf, sem, step, slot, steps_per_seq):
    g_pages = PAGES_PER_STEP
    seq = lax.div(step, steps_per_seq)
    first = lax.rem(step, steps_per_seq) * g_pages
    copies = []
    for g in range(g_pages):
        page = pt_ref[seq, first + g]
        copies.append(pltpu.make_async_copy(lat_hbm.at[0, page], lat_buf.at[slot, g], sem.at[slot, 0]))
        copies.append(pltpu.make_async_copy(rope_hbm.at[0, page], rope_buf.at[slot, g], sem.at[slot, 1]))
    return copies


def _decode_kernel(pt_ref, qlat_ref, qrope_ref, latn_ref, ropen_ref, lat_hbm, rope_hbm, o_ref,
                   lat_buf, rope_buf, sem, m_sc, l_sc, acc_sc, *, n_seq, nj):
    g_pages = PAGES_PER_STEP
    j = pl.program_id(1)
    step = pl.program_id(0) * nj + j
    n_steps = n_seq * nj
    slot = lax.rem(step, 2)
    copies = functools.partial(_page_copies, pt_ref, lat_hbm, rope_hbm, lat_buf, rope_buf, sem,
                               steps_per_seq=nj)

    @pl.when(step == 0)
    def _():
        for c in copies(step=step, slot=slot):
            c.start()

    @pl.when(step + 1 < n_steps)
    def _():
        for c in copies(step=step + 1, slot=1 - slot):
            c.start()

    for c in copies(step=step, slot=slot):
        c.wait()

    @pl.when(j == 0)
    def _():
        m_sc[...] = jnp.full(m_sc.shape, NEG_BIG, F32)
        l_sc[...] = jnp.zeros(l_sc.shape, F32)
        acc_sc[...] = jnp.zeros(acc_sc.shape, F32)

    ql = qlat_ref[0]
    qr = qrope_ref[0]
    per_chain = g_pages // DECODE_CHAINS
    chains = []
    for c in range(DECODE_CHAINS):
        lats = []
        ss = []
        for g in range(c * per_chain, (c + 1) * per_chain):
            lat = lat_buf[slot, g].astype(BF16)
            rpt = rope_buf[slot, g].astype(BF16)
            lats.append(lat)
            ss.append(_dot_nt(ql, lat) + _dot(qr, rpt))
        chains.append((jnp.concatenate(ss, axis=1), lats))
    m_run = m_sc[...]
    l_run = l_sc[...]
    acc = acc_sc[...]
    for s, lats in chains:
        m_next = jnp.maximum(m_run, jnp.max(s, axis=1, keepdims=True))
        alpha = jnp.exp2(m_run - m_next)
        p = jnp.exp2(s - m_next[:, 0:1])
        l_run = alpha * l_run + jnp.sum(p, axis=1, keepdims=True)
        pb = p.astype(BF16)
        pv = _dot(pb[:, 0:PAGE_SIZE], lats[0])
        for g in range(1, per_chain):
            pv = pv + _dot(pb[:, g * PAGE_SIZE:(g + 1) * PAGE_SIZE], lats[g])
        acc = acc * alpha[:, 0:1] + pv
        m_run = m_next
    m_sc[...] = m_run
    l_sc[...] = l_run
    acc_sc[...] = acc

    @pl.when(j == nj - 1)
    def _():
        qlf = ql.astype(F32)
        qrf = qr.astype(F32)
        s_new = (jnp.sum(qlf * latn_ref[0], axis=1, keepdims=True)
                 + jnp.sum(qrf * ropen_ref[0], axis=1, keepdims=True))
        m_old = m_sc[...][:, 0:1]
        m_fin = jnp.maximum(m_old, s_new)
        a_old = jnp.exp2(m_old - m_fin)
        p_new = jnp.exp2(s_new - m_fin)
        l_fin = a_old * l_sc[...][:, 0:1] + p_new
        o_ref[0] = (acc_sc[...] * a_old + p_new * latn_ref[0]) / l_fin


def _decode(page_table, qlat, qrope, lat_new, rope_new, cache_lat, cache_rope):
    b = qlat.shape[0]
    n_pages = page_table.shape[1]
    g_pages = PAGES_PER_STEP
    in_specs = [
        pl.BlockSpec((1, N_HEADS, KV_RANK), lambda s, j, pt: (s, 0, 0)),
        pl.BlockSpec((1, N_HEADS, QK_ROPE), lambda s, j, pt: (s, 0, 0)),
        pl.BlockSpec((1, 1, KV_RANK), lambda s, j, pt: (s, 0, 0)),
        pl.BlockSpec((1, 1, QK_ROPE), lambda s, j, pt: (s, 0, 0)),
        pl.BlockSpec(memory_space=pl.ANY),
        pl.BlockSpec(memory_space=pl.ANY),
    ]
    assert n_pages % g_pages == 0
    grid_spec = pltpu.PrefetchScalarGridSpec(
        num_scalar_prefetch=1,
        grid=(b, n_pages // g_pages),
        in_specs=in_specs,
        out_specs=pl.BlockSpec((1, N_HEADS, KV_RANK), lambda s, j, pt: (s, 0, 0)),
        scratch_shapes=[
            pltpu.VMEM((2, g_pages, PAGE_SIZE, KV_RANK), F32),
            pltpu.VMEM((2, g_pages, QK_ROPE, PAGE_SIZE), F32),
            pltpu.SemaphoreType.DMA((2, 2)),
            pltpu.VMEM((N_HEADS, LANES), F32),
            pltpu.VMEM((N_HEADS, LANES), F32),
            pltpu.VMEM((N_HEADS, KV_RANK), F32),
        ],
    )
    return pl.pallas_call(
        functools.partial(_decode_kernel, n_seq=b, nj=n_pages // g_pages), grid_spec=grid_spec,
        out_shape=jax.ShapeDtypeStruct((b, N_HEADS, KV_RANK), F32),
        compiler_params=pltpu.CompilerParams(dimension_semantics=("arbitrary", "arbitrary"),
                                             vmem_limit_bytes=VMEM_LIMIT),
        name="sample_decode",
    )(page_table, qlat, qrope, lat_new, rope_new, cache_lat, cache_rope)


def _spost_kernel(x_ref, olat_ref, cvn_ref, mod_ref, wuvbd_ref, ona_ref, wout_ref, norm2_ref, wup_ref,
                  fcw_ref, fcb_ref, wdn_ref, fn_ref, st_ref, y_out, st_out):
    mod = mod_ref[...]
    g1 = mod[:, 2 * D_MODEL:3 * D_MODEL]
    sh2 = mod[:, 3 * D_MODEL:4 * D_MODEL]
    sc2 = mod[:, 4 * D_MODEL:5 * D_MODEL]
    g2 = mod[:, 5 * D_MODEL:6 * D_MODEL]
    attn = _dot(olat_ref[...].astype(BF16), wuvbd_ref[...])
    an = _rms(attn, ona_ref[...])
    x1 = _mix_residual(x_ref[...], an.astype(BF16), cvn_ref[...], g1, wout_ref)
    h2 = (_rms(x1, norm2_ref[...]) * (1.0 + sc2) + sh2).astype(BF16)
    u = _dot(h2, wup_ref[...])
    w = 2 * D_FF
    up = fcb_ref[...] + fcw_ref[0:1, :] * st_ref[:, 0:w] + fcw_ref[1:2, :] * st_ref[:, w:2 * w] + fcw_ref[2:3, :] * u
    st_out[:, 0:w] = st_ref[:, w:2 * w]
    st_out[:, w:2 * w] = u
    act = (_silu(up[:, 0:D_FF]) * up[:, D_FF:w]).astype(BF16)
    x2 = x1 + g2 * _dot(act, wdn_ref[...])
    y_out[...] = _rms(x2, fn_ref[...])


def _sample_post(x, olat, cvn, mod_s, wuvbd, ona, wout, norm2, wup, fcw, fcb, wdn, fnorm, st):
    b = x.shape[0]
    out_shape = (jax.ShapeDtypeStruct((b, D_MODEL), F32), jax.ShapeDtypeStruct(st.shape, F32))
    return pl.pallas_call(
        _spost_kernel, out_shape=out_shape,
        compiler_params=pltpu.CompilerParams(vmem_limit_bytes=VMEM_LIMIT),
        name="sample_post",
    )(x, olat, cvn, mod_s, wuvbd, ona, wout, norm2, wup, fcw, fcb, wdn, fnorm, st)


def _rope_tables(pos):
    freqs = ROPE_THETA ** (-jnp.arange(ROPE_HALF, dtype=F32) / ROPE_HALF)
    ang = pos.astype(F32)[:, None] * freqs[None, :]
    cos = jnp.cos(ang)
    sin = jnp.sin(ang)
    s = pos.shape[0]
    c2 = jnp.concatenate([cos, cos], axis=1)
    s2 = jnp.concatenate([sin, sin], axis=1)
    zpad = jnp.zeros((s, HEAD_PAD - QK_ROPE), F32)
    qscale = ATTN_SCALE * LOG2E
    cosq = jnp.concatenate([c2, jnp.ones((s, QK_NOPE), F32), jnp.zeros((s, HEAD_PAD - QK_ROPE - QK_NOPE), F32)],
                           axis=1) * qscale
    sinq = jnp.concatenate([s2, zpad], axis=1) * qscale
    cosk = jnp.concatenate([c2, zpad], axis=1)
    sink = jnp.concatenate([s2, zpad], axis=1)
    return jnp.stack([cosq, sinq, cosk, sink], axis=0)


def _rot_half_cols(w):
    return jnp.concatenate([-w[..., ROPE_HALF:], w[..., :ROPE_HALF]], axis=-1)


def _pack_weights(w_in, w_uq, w_uk, w_uv):
    o1, o2, o3 = Q_RANK, Q_RANK + KV_RANK, Q_RANK + KV_RANK + QK_ROPE
    wkr = w_in[:, o2:o3]
    zpad = jnp.zeros((D_MODEL, HEAD_PAD - QK_ROPE), F32)
    wz = jnp.concatenate([w_in[:, :o2], w_in[:, o3:], wkr, zpad, _rot_half_cols(wkr), zpad], axis=1).astype(BF16)

    wq = w_uq.reshape(Q_RANK, N_HEADS, QK_NOPE + QK_ROPE)
    wq_nope, wq_rope = wq[..., :QK_NOPE], wq[..., QK_NOPE:]
    ztail = jnp.zeros((Q_RANK, N_HEADS, HEAD_PAD - QK_ROPE - QK_NOPE), F32)
    wq_plain = jnp.concatenate([wq_rope, wq_nope, ztail], axis=2).reshape(Q_RANK, N_HEADS * HEAD_PAD)
    wq_rot = jnp.concatenate([_rot_half_cols(wq_rope), jnp.zeros((Q_RANK, N_HEADS, HEAD_PAD - QK_ROPE), F32)],
                             axis=2).reshape(Q_RANK, N_HEADS * HEAD_PAD)
    wq2 = jnp.concatenate([wq_plain, wq_rot], axis=1).astype(BF16)

    wuk = jnp.concatenate([jnp.zeros((KV_RANK, N_HEADS, QK_ROPE), F32), w_uk,
                           jnp.zeros((KV_RANK, N_HEADS, HEAD_PAD - QK_ROPE - QK_NOPE), F32)],
                          axis=2).reshape(KV_RANK, N_HEADS * HEAD_PAD).astype(BF16)
    wuvt = jnp.transpose(w_uv, (1, 2, 0))
    wuvt = jnp.concatenate([wuvt, jnp.zeros((N_HEADS, V_EXT - V_DIM, KV_RANK), F32)],
                           axis=1).reshape(N_HEADS * V_EXT, KV_RANK).astype(BF16)

    wukt = jnp.transpose(w_uk, (1, 2, 0))
    wukt = jnp.concatenate([jnp.zeros((N_HEADS, QK_ROPE, KV_RANK), F32), wukt,
                            jnp.zeros((N_HEADS, HEAD_PAD - QK_ROPE - QK_NOPE, KV_RANK), F32)], axis=1)
    eye = jnp.eye(N_HEADS, dtype=F32)
    wukt_bd = (eye[:, None, :, None] * wukt[:, :, None, :]).reshape(N_HEADS * HEAD_PAD, N_HEADS * KV_RANK).astype(BF16)
    wuvh = jnp.transpose(w_uv, (1, 0, 2))
    wuv_bd = (eye[:, None, :, None] * wuvh[:, :, None, :]).reshape(N_HEADS * KV_RANK, ATTN_WIDTH).astype(BF16)
    return wz, wq2, wuk, wuvt, wukt_bd, wuv_bd


def kernel(x_prompt, x_sample, c_prompt, c_sample, cache_kv_latent, cache_k_rope, state_conv, state_ffn_conv, page_table, w_ada, b_ada, norm1, w_in, q_norm, w_uq, kv_norm, w_uk, w_uv, conv_w, conv_b, conv_ln_g, conv_ln_b, out_norm_attn, out_norm_conv, w_out, norm2, w_up, ffn_conv_w, ffn_conv_b, w_down, final_norm):
    assert w_ada.shape[0] == 1, "single-layer decoder"
    n_p, s_p, _ = x_prompt.shape
    n_s, s_s, _ = x_sample.shape
    assert s_s == 1

    row = lambda a: a.reshape(1, -1)
    wz, wq2, wuk, wuvt, wukt_bd, wuv_bd = _pack_weights(w_in[0], w_uq[0], w_uk[0], w_uv[0])
    wout = w_out[0].astype(BF16)
    wup = w_up[0].astype(BF16)
    wdn = w_down[0].astype(BF16)
    norm1_r, qn_r, kvn_r = row(norm1[0]), row(q_norm[0]), row(kv_norm[0])
    cw, cb = conv_w[0], row(conv_b[0])
    lng, lnb, onc, ona = row(conv_ln_g[0]), row(conv_ln_b[0]), row(out_norm_conv[0]), row(out_norm_attn[0])
    norm2_r, fcw, fcb, fnorm = row(norm2[0]), ffn_conv_w[0], row(ffn_conv_b[0]), row(final_norm)

    n_c = n_p + n_s
    n_c_pad = -(-n_c // SUBLANES) * SUBLANES
    c_all = jnp.concatenate([c_prompt, c_sample, jnp.zeros((n_c_pad - n_c, D_MODEL), F32)], axis=0)
    mod = _modulation(c_all, w_ada[0], b_ada[0])
    mod_p = mod[:n_p].reshape(n_p, N_MOD, D_MODEL)
    mod_s = mod[n_p:n_c]

    tab_p = _rope_tables(jnp.arange(s_p))
    tabq_t = jnp.transpose(tab_p[0:2], (0, 2, 1))
    qt, k, vt, kv_lat_p, k_rope_p, cvn_p, conv_st_p = _prompt_pre(
        x_prompt, mod_p, tabq_t, tab_p[2:4], norm1_r, wz, qn_r, wq2.T, kvn_r, wuk, wuvt, cw, cb, lng, lnb, onc)
    attn_p = _flash(qt, k, vt)
    y_p, ffn_st_p = _prompt_post(x_prompt, attn_p, cvn_p, mod_p, ona, wout, norm2_r, wup, fcw, fcb, wdn, fnorm)

    tab_s = _rope_tables(PAST_LEN + jnp.arange(s_s))
    st_conv = state_conv[0].reshape(n_s, (CONV_K - 1) * CONV_WIDTH)
    qlat, q_s, lat_s, kr_s, cvn_s, st_conv_new = _sample_pre(
        x_sample[:, 0], mod_s, tab_s, norm1_r, wz, qn_r, wq2, kvn_r, wukt_bd, cw, cb, lng, lnb, onc, st_conv)
    qrope = q_s.reshape(n_s, N_HEADS, HEAD_PAD)[:, :, :QK_ROPE]
    o_lat = _decode(page_table, qlat.reshape(n_s, N_HEADS, KV_RANK), qrope,
                    lat_s.reshape(n_s, 1, KV_RANK), kr_s.reshape(n_s, 1, QK_ROPE),
                    cache_kv_latent, jnp.swapaxes(cache_k_rope, 2, 3))
    st_ffn = state_ffn_conv[0].reshape(n_s, (FFN_K - 1) * 2 * D_FF)
    y_s, st_ffn_new = _sample_post(
        x_sample[:, 0], o_lat.reshape(n_s, N_HEADS * KV_RANK), cvn_s, mod_s, wuv_bd, ona, wout, norm2_r,
        wup, fcw, fcb, wdn, fnorm, st_ffn)

    return (
        y_p,
        y_s.reshape(n_s, 1, D_MODEL),
        kv_lat_p,
        k_rope_p,
        conv_st_p,
        ffn_st_p,
        lat_s.reshape(1, n_s, 1, KV_RANK),
        kr_s.reshape(1, n_s, 1, QK_ROPE),
        st_conv_new.reshape(1, n_s, CONV_K - 1, CONV_WIDTH),
        st_ffn_new.reshape(1, n_s, FFN_K - 1, 2 * D_FF),
    )
```

```python
import functools
import math

import jax
import jax.numpy as jnp
from jax import lax
from jax.experimental import pallas as pl
from jax.experimental.pallas import tpu as pltpu

D_MODEL = 1024
N_HEADS = 8
QK_NOPE = 64
QK_ROPE = 32
V_DIM = 64
Q_RANK = 384
KV_RANK = 256
ATTN_WIDTH = N_HEADS * V_DIM
CONV_WIDTH = 512
CONV_K = 31
D_FF = 2816
FFN_K = 3
N_MOD = 6
ROPE_THETA = 10000.0
EPS = 1e-6
PAST_LEN = 16384
PAGE_SIZE = 128
ATTN_SCALE = (QK_NOPE + QK_ROPE) ** -0.5
LOG2E = math.log2(math.e)

LANES = 128
SUBLANES = 8
HEAD_PAD = LANES
V_EXT = V_DIM + 16
ROPE_HALF = QK_ROPE // 2
NEG_BIG = -1e30

TM_PRE = 512
TM_POST = 512
FF_CHUNK = 256
TQ = 2048
TK = 512
FLASH_LOOKAHEAD = 2
PAGES_PER_STEP = 64
DECODE_CHAINS = 2
CARRY_ROWS = 32
VMEM_LIMIT = 56 * 1024 * 1024

Z_CQ = 0
Z_CKV = Z_CQ + Q_RANK
Z_GA = Z_CKV + KV_RANK
Z_GB = Z_GA + CONV_WIDTH
Z_KR = Z_GB + CONV_WIDTH
Z_KRR = Z_KR + HEAD_PAD
Z_END = Z_KRR + HEAD_PAD

BF16 = jnp.bfloat16
F32 = jnp.float32


def _dot(a, b):
    return jnp.dot(a, b, preferred_element_type=F32)


def _dot_nt(a, b):
    return lax.dot_general(a, b, (((1,), (1,)), ((), ())), preferred_element_type=F32)


def _rms(x, g):
    return x * lax.rsqrt(jnp.mean(x * x, axis=-1, keepdims=True) + EPS) * g


def _silu(x):
    return x * jax.nn.sigmoid(x)


def _mod_kernel(c_ref, w_ref, b_ref, o_ref):
    s = _silu(c_ref[...])
    o_ref[...] = _dot(s.astype(BF16), w_ref[...].astype(BF16)) + b_ref[...]


def _modulation(c, w_ada, b_ada):
    n = c.shape[0]
    tn = 1024
    return pl.pallas_call(
        _mod_kernel,
        grid=(N_MOD * D_MODEL // tn,),
        in_specs=[
            pl.BlockSpec((n, D_MODEL), lambda j: (0, 0)),
            pl.BlockSpec((D_MODEL, tn), lambda j: (0, j)),
            pl.BlockSpec((1, tn), lambda j: (0, j)),
        ],
        out_specs=pl.BlockSpec((n, tn), lambda j: (0, j)),
        out_shape=jax.ShapeDtypeStruct((n, N_MOD * D_MODEL), F32),
        compiler_params=pltpu.CompilerParams(dimension_semantics=("arbitrary",), vmem_limit_bytes=VMEM_LIMIT),
        name="adaln_mod",
    )(c, w_ada, b_ada.reshape(1, -1))


def _front(x, sh1, sc1, norm1, wz_ref, qn, kvn, cosk, sink):
    h = _rms(x, norm1) * (1.0 + sc1) + sh1
    z = _dot(h.astype(BF16), wz_ref[...])
    cqn = _rms(z[:, Z_CQ:Z_CKV], qn).astype(BF16)
    lat = _rms(z[:, Z_CKV:Z_GA], kvn)
    kro = z[:, Z_KR:Z_KRR] * cosk + z[:, Z_KRR:Z_END] * sink
    glu = z[:, Z_GA:Z_GB] * jax.nn.sigmoid(z[:, Z_GB:Z_KR])
    return cqn, lat, kro, glu


def _conv_tail(cv, lng, lnb, onc):
    mu = jnp.mean(cv, axis=-1, keepdims=True)
    d = cv - mu
    y = d * lax.rsqrt(jnp.mean(d * d, axis=-1, keepdims=True) + EPS) * lng + lnb
    return _rms(_silu(y), onc)


def _pre_kernel(x_ref, mod_ref, tabq_ref, tabk_ref, norm1_ref, wz_ref, qn_ref, wq2t_ref, kvn_ref, wuk_ref, wuvt_ref,
                cw_ref, cb_ref, lng_ref, lnb_ref, onc_ref,
                qt_out, k_out, vt_out, lat_out, kr_out, cvn_out, cst_out,
                ext_ref, shift_ref):
    i = pl.program_id(1)
    tm = x_ref.shape[1]
    m = mod_ref[0]
    cqn, lat, kro, glu = _front(x_ref[0], m[0:1], m[1:2], norm1_ref[...], wz_ref, qn_ref[...],
                                kvn_ref[...], tabk_ref[0], tabk_ref[1])
    q2t = _dot_nt(wq2t_ref[...], cqn)
    cosqt = tabq_ref[0]
    sinqt = tabq_ref[1]
    hw = N_HEADS * HEAD_PAD
    for hd in range(N_HEADS):
        lo = hd * HEAD_PAD
        qt_out[0, 0, lo:lo + HEAD_PAD, :] = (q2t[lo:lo + HEAD_PAD, :] * cosqt
                                             + q2t[hw + lo:hw + lo + HEAD_PAD, :] * sinqt).astype(BF16)
    lat_out[0, 0] = lat
    kr_out[0, 0] = kro[:, :QK_ROPE]
    latb = lat.astype(BF16)
    kn = _dot(latb, wuk_ref[...])
    for hd in range(N_HEADS):
        k_out[0, :, hd * HEAD_PAD:(hd + 1) * HEAD_PAD] = (kn[:, hd * HEAD_PAD:(hd + 1) * HEAD_PAD] + kro).astype(BF16)
    vt = _dot_nt(wuvt_ref[...], latb)
    vrow = lax.broadcasted_iota(jnp.int32, vt.shape, 0)
    is_one = functools.reduce(jnp.logical_or, [vrow == hd * V_EXT + V_DIM for hd in range(N_HEADS)])
    vt_out[0, 0] = jnp.where(is_one, 1.0, vt).astype(BF16)

    @pl.when(i == 0)
    def _():
        ext_ref[0:CARRY_ROWS, :] = jnp.zeros((CARRY_ROWS, CONV_WIDTH), F32)

    ext_ref[CARRY_ROWS:CARRY_ROWS + tm, :] = glu
    base = CARRY_ROWS - (CONV_K - 1)
    cv = jnp.broadcast_to(cb_ref[...], (tm, CONV_WIDTH))
    for r in range(SUBLANES):
        taps = [k for k in range(CONV_K) if (base + k) % SUBLANES == r]
        if not taps:
            continue
        span = max((base + k) // SUBLANES for k in taps) * SUBLANES
        if r:
            shift_ref[0:tm + span, :] = ext_ref[pl.ds(r, tm + span), :]
        src = shift_ref if r else ext_ref
        for k in taps:
            a = (base + k) // SUBLANES * SUBLANES
            cv = cv + cw_ref[k:k + 1, :] * src[pl.ds(a, tm), :]
    ext_ref[0:CARRY_ROWS, :] = ext_ref[tm:tm + CARRY_ROWS, :]

    @pl.when(i == pl.num_programs(1) - 1)
    def _():
        cst_out[0, 0] = ext_ref[base:CARRY_ROWS, :]

    cvn_out[0] = _conv_tail(cv, lng_ref[...], lnb_ref[...], onc_ref[...]).astype(BF16)


def _const_spec(shape):
    nd = len(shape)
    return pl.BlockSpec(shape, lambda *_: (0,) * nd, pipeline_mode=pl.Buffered(1))


def _prompt_pre(x, mod_p, tabq, tabk, norm1, wz, qn, wq2t, kvn, wuk, wuvt, cw, cb, lng, lnb, onc):
    n, s, _ = x.shape
    tm = TM_PRE
    grid = (n, s // tm)
    tok = lambda w: pl.BlockSpec((1, tm, w), lambda b, i: (b, i, 0))
    tok_t = lambda w: pl.BlockSpec((1, 1, w, tm), lambda b, i: (b, i, 0, 0))
    in_specs = [
        tok(D_MODEL),
        pl.BlockSpec((1, N_MOD, D_MODEL), lambda b, i: (b, 0, 0)),
        pl.BlockSpec((2, HEAD_PAD, tm), lambda b, i: (0, 0, i)),
        pl.BlockSpec((2, tm, LANES), lambda b, i: (0, i, 0)),
        _const_spec(norm1.shape), _const_spec(wz.shape), _const_spec(qn.shape), _const_spec(wq2t.shape),
        _const_spec(kvn.shape), _const_spec(wuk.shape), _const_spec(wuvt.shape),
        _const_spec(cw.shape), _const_spec(cb.shape), _const_spec(lng.shape), _const_spec(lnb.shape),
        _const_spec(onc.shape),
    ]
    out_shape = (
        jax.ShapeDtypeStruct((n, s // tm, N_HEADS * HEAD_PAD, tm), BF16),
        jax.ShapeDtypeStruct((n, s, N_HEADS * HEAD_PAD), BF16),
        jax.ShapeDtypeStruct((n, s // tm, N_HEADS * V_EXT, tm), BF16),
        jax.ShapeDtypeStruct((1, n, s, KV_RANK), F32),
        jax.ShapeDtypeStruct((1, n, s, QK_ROPE), F32),
        jax.ShapeDtypeStruct((n, s, CONV_WIDTH), BF16),
        jax.ShapeDtypeStruct((1, n, CONV_K - 1, CONV_WIDTH), F32),
    )
    out_specs = (
        tok_t(N_HEADS * HEAD_PAD), tok(N_HEADS * HEAD_PAD), tok_t(N_HEADS * V_EXT),
        pl.BlockSpec((1, 1, tm, KV_RANK), lambda b, i: (0, b, i, 0)),
        pl.BlockSpec((1, 1, tm, QK_ROPE), lambda b, i: (0, b, i, 0)),
        tok(CONV_WIDTH),
        pl.BlockSpec((1, 1, CONV_K - 1, CONV_WIDTH), lambda b, i: (0, b, 0, 0)),
    )
    return pl.pallas_call(
        _pre_kernel, grid=grid, in_specs=in_specs, out_specs=out_specs, out_shape=out_shape,
        scratch_shapes=[pltpu.VMEM((CARRY_ROWS + tm, CONV_WIDTH), F32),
                        pltpu.VMEM((CARRY_ROWS + tm, CONV_WIDTH), F32)],
        compiler_params=pltpu.CompilerParams(dimension_semantics=("arbitrary", "arbitrary"),
                                             vmem_limit_bytes=VMEM_LIMIT),
        name="prompt_pre",
    )(x, mod_p, tabq, tabk, norm1, wz, qn, wq2t, kvn, wuk, wuvt, cw, cb, lng, lnb, onc)


def _flash_kernel(qi_ref, ki_ref, qt_ref, k_ref, vt_ref, o_ref, m_sc, acc_sc):
    t = pl.program_id(1)
    qi = qi_ref[t]
    ki = ki_ref[t]
    n_groups, wq = qt_ref.shape[1], qt_ref.shape[3]
    tq = n_groups * wq
    tk = k_ref.shape[1]

    @pl.when(ki == 0)
    def _():
        m_sc[...] = jnp.full(m_sc.shape, NEG_BIG, F32)
        acc_sc[...] = jnp.zeros(acc_sc.shape, F32)

    units = [(hd, c * wq) for hd in range(N_HEADS) for c in range(n_groups)]

    def scores(unit):
        hd, q0 = unit
        qt = qt_ref[0, q0 // wq, hd * HEAD_PAD:(hd + 1) * HEAD_PAD, :]
        k = k_ref[0, :, hd * HEAD_PAD:(hd + 1) * HEAD_PAD]
        return _dot(k, qt)

    def step(key0):
        live = units
        if key0 is not None:
            key = lax.broadcasted_iota(jnp.int32, (tk, wq), 0)
            qry = lax.broadcasted_iota(jnp.int32, (tk, wq), 1)
            live = [(hd, q0) for hd, q0 in units if q0 + wq > key0]
        pending = [scores(u) for u in live[:FLASH_LOOKAHEAD]]
        for i, (hd, q0) in enumerate(live):
            st = pending.pop(0)
            if i + FLASH_LOOKAHEAD < len(live):
                pending.append(scores(live[i + FLASH_LOOKAHEAD]))
            if key0 is not None and q0 < key0 + tk - 1:
                st = jnp.where(key + key0 <= qry + q0, st, NEG_BIG)
            vt = vt_ref[0, 0, hd * V_EXT:(hd + 1) * V_EXT, :]
            m_prev = m_sc[hd, :, q0:q0 + wq]
            m_next = jnp.maximum(m_prev, jnp.max(st, axis=0, keepdims=True))
            alpha = jnp.exp2(m_prev - m_next)
            p = jnp.exp2(st - m_next)
            acc_sc[hd, :, q0:q0 + wq] = acc_sc[hd, :, q0:q0 + wq] * alpha + _dot(vt, p.astype(BF16))
            m_sc[hd, :, q0:q0 + wq] = m_next

    ratio = tq // tk

    @pl.when(ki < qi * ratio)
    def _():
        step(None)

    for d in range(ratio):
        @pl.when(ki == qi * ratio + d)
        def _(d=d):
            step(d * tk)

    @pl.when(ki == qi * ratio + ratio - 1)
    def _():
        for pr in range(N_HEADS // 2):
            halves = []
            for hd in (2 * pr, 2 * pr + 1):
                inv = 1.0 / acc_sc[hd, V_DIM:V_DIM + 1, :]
                halves.append(acc_sc[hd, 0:V_DIM, :] * inv)
            o_ref[0, :, pr * LANES:(pr + 1) * LANES] = jnp.concatenate(halves, axis=0).T


def _flash(qt, k, vt):
    n, s, _ = k.shape
    tq, tk = TQ, TK
    tile = qt.shape[3]
    assert tq % tk == 0 and s % tq == 0 and tq % tile == 0 and tk == tile
    pairs = [(a, b) for a in range(s // tq) for b in range((a + 1) * (tq // tk))]
    qi = jnp.asarray([p[0] for p in pairs], jnp.int32)
    ki = jnp.asarray([p[1] for p in pairs], jnp.int32)
    grid_spec = pltpu.PrefetchScalarGridSpec(
        num_scalar_prefetch=2,
        grid=(n, len(pairs)),
        in_specs=[
            pl.BlockSpec((1, tq // tile, N_HEADS * HEAD_PAD, tile), lambda b, t, qi, ki: (b, qi[t], 0, 0)),
            pl.BlockSpec((1, tk, N_HEADS * HEAD_PAD), lambda b, t, qi, ki: (b, ki[t], 0)),
            pl.BlockSpec((1, 1, N_HEADS * V_EXT, tk), lambda b, t, qi, ki: (b, ki[t], 0, 0)),
        ],
        out_specs=pl.BlockSpec((1, tq, ATTN_WIDTH), lambda b, t, qi, ki: (b, qi[t], 0)),
        scratch_shapes=[
            pltpu.VMEM((N_HEADS, 1, tq), F32),
            pltpu.VMEM((N_HEADS, V_EXT, tq), F32),
        ],
    )
    return pl.pallas_call(
        _flash_kernel, grid_spec=grid_spec,
        out_shape=jax.ShapeDtypeStruct((n, s, ATTN_WIDTH), F32),
        compiler_params=pltpu.CompilerParams(dimension_semantics=("arbitrary", "arbitrary"),
                                             vmem_limit_bytes=VMEM_LIMIT),
        name="prompt_flash",
    )(qi, ki, qt, k, vt)


def _mix_residual(x, an_b, cvn_b, g1, wout_ref):
    mix = _dot(an_b, wout_ref[0:ATTN_WIDTH, :]) + _dot(cvn_b, wout_ref[ATTN_WIDTH:, :])
    return x + g1 * mix


def _post_kernel(x_ref, attn_ref, cvn_ref, mod_ref, ona_ref, wout_ref, norm2_ref, wup_ref, fcw_ref, fcb_ref,
                 wdn_ref, fn_ref, y_out, fst_out, ext_ref, carry_ref):
    i = pl.program_id(1)
    tm = x_ref.shape[1]
    m = mod_ref[0]
    g1, sh2, sc2, g2 = m[2:3], m[3:4], m[4:5], m[5:6]
    an = _rms(attn_ref[0], ona_ref[...])
    x1 = _mix_residual(x_ref[0], an.astype(BF16), cvn_ref[0], g1, wout_ref)
    h2 = (_rms(x1, norm2_ref[...]) * (1.0 + sc2) + sh2).astype(BF16)

    @pl.when(i == 0)
    def _():
        carry_ref[...] = jnp.zeros(carry_ref.shape, F32)

    def up_proj(c):
        return (_dot(h2, wup_ref[:, c * FF_CHUNK:(c + 1) * FF_CHUNK]),
                _dot(h2, wup_ref[:, D_FF + c * FF_CHUNK:D_FF + (c + 1) * FF_CHUNK]))

    def conv_part(u, off):
        ext_ref[0:SUBLANES, :] = carry_ref[:, off:off + FF_CHUNK]
        ext_ref[SUBLANES:SUBLANES + tm, :] = u
        carry_ref[:, off:off + FF_CHUNK] = ext_ref[tm:tm + SUBLANES, :]
        up = fcb_ref[:, off:off + FF_CHUNK] + fcw_ref[2:3, off:off + FF_CHUNK] * u
        for k in range(FFN_K - 1):
            up = up + fcw_ref[k:k + 1, off:off + FF_CHUNK] * ext_ref[pl.ds(SUBLANES - (FFN_K - 1) + k, tm), :]
        return up

    n_chunks = D_FF // FF_CHUNK
    ffn = jnp.zeros((tm, D_MODEL), F32)
    u_next = up_proj(0)
    for c in range(n_chunks):
        ua, uv = u_next
        if c + 1 < n_chunks:
            u_next = up_proj(c + 1)
        a = conv_part(ua, c * FF_CHUNK)
        v = conv_part(uv, D_FF + c * FF_CHUNK)
        act = (_silu(a) * v).astype(BF16)
        ffn = ffn + _dot(act, wdn_ref[c * FF_CHUNK:(c + 1) * FF_CHUNK, :])
    x2 = x1 + g2 * ffn
    y_out[0] = _rms(x2, fn_ref[...])

    @pl.when(i == pl.num_programs(1) - 1)
    def _():
        fst_out[0, 0] = carry_ref[SUBLANES - (FFN_K - 1):SUBLANES, :]


def _prompt_post(x, attn, cvn, mod_p, ona, wout, norm2, wup, fcw, fcb, wdn, fnorm):
    n, s, _ = x.shape
    tm = TM_POST
    tok = lambda w: pl.BlockSpec((1, tm, w), lambda b, i: (b, i, 0))
    in_specs = [
        tok(D_MODEL), tok(ATTN_WIDTH), tok(CONV_WIDTH),
        pl.BlockSpec((1, N_MOD, D_MODEL), lambda b, i: (b, 0, 0)),
        _const_spec(ona.shape), _const_spec(wout.shape), _const_spec(norm2.shape), _const_spec(wup.shape),
        _const_spec(fcw.shape), _const_spec(fcb.shape), _const_spec(wdn.shape), _const_spec(fnorm.shape),
    ]
    out_shape = (
        jax.ShapeDtypeStruct((n, s, D_MODEL), F32),
        jax.ShapeDtypeStruct((1, n, FFN_K - 1, 2 * D_FF), F32),
    )
    out_specs = (
        tok(D_MODEL),
        pl.BlockSpec((1, 1, FFN_K - 1, 2 * D_FF), lambda b, i: (0, b, 0, 0)),
    )
    return pl.pallas_call(
        _post_kernel, grid=(n, s // tm), in_specs=in_specs, out_specs=out_specs, out_shape=out_shape,
        scratch_shapes=[pltpu.VMEM((SUBLANES + tm, FF_CHUNK), F32), pltpu.VMEM((SUBLANES, 2 * D_FF), F32)],
        compiler_params=pltpu.CompilerParams(dimension_semantics=("arbitrary", "arbitrary"),
                                             vmem_limit_bytes=VMEM_LIMIT),
        name="prompt_post",
    )(x, attn, cvn, mod_p, ona, wout, norm2, wup, fcw, fcb, wdn, fnorm)


def _spre_kernel(x_ref, mod_ref, tab_ref, norm1_ref, wz_ref, qn_ref, wq2_ref, kvn_ref, wukt_ref,
                 cw_ref, cb_ref, lng_ref, lnb_ref, onc_ref, st_ref,
                 qlat_out, q_out, lat_out, kr_out, cvn_out, st_out):
    mod = mod_ref[...]
    sh1 = mod[:, 0:D_MODEL]
    sc1 = mod[:, D_MODEL:2 * D_MODEL]
    cqn, lat, kro, glu = _front(x_ref[...], sh1, sc1, norm1_ref[...], wz_ref, qn_ref[...],
                                kvn_ref[...], tab_ref[2], tab_ref[3])
    q2 = _dot(cqn, wq2_ref[...])
    qs = []
    for hd in range(N_HEADS):
        a = q2[:, hd * HEAD_PAD:(hd + 1) * HEAD_PAD]
        b = q2[:, (N_HEADS + hd) * HEAD_PAD:(N_HEADS + hd + 1) * HEAD_PAD]
        qs.append(a * tab_ref[0] + b * tab_ref[1])
    q = jnp.concatenate(qs, axis=1).astype(BF16)
    q_out[...] = q
    qlat_out[...] = _dot(q, wukt_ref[...]).astype(BF16)
    lat_out[...] = lat
    kr_out[...] = kro[:, :QK_ROPE]
    kst = CONV_K - 1
    cv = cb_ref[...] + cw_ref[kst:kst + 1, :] * glu
    for k in range(kst):
        cv = cv + cw_ref[k:k + 1, :] * st_ref[:, k * CONV_WIDTH:(k + 1) * CONV_WIDTH]
    st_out[:, 0:(kst - 1) * CONV_WIDTH] = st_ref[:, CONV_WIDTH:kst * CONV_WIDTH]
    st_out[:, (kst - 1) * CONV_WIDTH:kst * CONV_WIDTH] = glu
    cvn_out[...] = _conv_tail(cv, lng_ref[...], lnb_ref[...], onc_ref[...]).astype(BF16)


def _sample_pre(x, mod_s, tab, norm1, wz, qn, wq2, kvn, wukt, cw, cb, lng, lnb, onc, st):
    b = x.shape[0]
    out_shape = (
        jax.ShapeDtypeStruct((b, N_HEADS * KV_RANK), BF16),
        jax.ShapeDtypeStruct((b, N_HEADS * HEAD_PAD), BF16),
        jax.ShapeDtypeStruct((b, KV_RANK), F32),
        jax.ShapeDtypeStruct((b, QK_ROPE), F32),
        jax.ShapeDtypeStruct((b, CONV_WIDTH), BF16),
        jax.ShapeDtypeStruct(st.shape, F32),
    )
    return pl.pallas_call(
        _spre_kernel, out_shape=out_shape,
        compiler_params=pltpu.CompilerParams(vmem_limit_bytes=VMEM_LIMIT),
        name="sample_pre",
    )(x, mod_s, tab, norm1, wz, qn, wq2, kvn, wukt, cw, cb, lng, lnb, onc, st)


def _page_copies(pt_ref, lat_hbm, rope_hbm, lat_buf, rope_buf, sem, step, slot, steps_per_seq):
    g_pages = PAGES_PER_STEP
    seq = lax.div(step, steps_per_seq)
    first = lax.rem(step, steps_per_seq) * g_pages
    copies = []
    for g in range(g_pages):
        page = pt_ref[seq, first + g]
        copies.append(pltpu.make_async_copy(lat_hbm.at[0, page], lat_buf.at[slot, g], sem.at[slot, 0]))
        copies.append(pltpu.make_async_copy(rope_hbm.at[0, page], rope_buf.at[slot, g], sem.at[slot, 1]))
    return copies


def _decode_kernel(pt_ref, qlat_ref, qrope_ref, latn_ref, ropen_ref, lat_hbm, rope_hbm, o_ref,
                   lat_buf, rope_buf, sem, m_sc, l_sc, acc_sc, *, n_seq, nj):
    g_pages = PAGES_PER_STEP
    j = pl.program_id(1)
    step = pl.program_id(0) * nj + j
    n_steps = n_seq * nj
    slot = lax.rem(step, 2)
    copies = functools.partial(_page_copies, pt_ref, lat_hbm, rope_hbm, lat_buf, rope_buf, sem,
                               steps_per_seq=nj)

    @pl.when(step == 0)
    def _():
        for c in copies(step=step, slot=slot):
            c.start()

    @pl.when(step + 1 < n_steps)
    def _():
        for c in copies(step=step + 1, slot=1 - slot):
            c.start()

    for c in copies(step=step, slot=slot):
        c.wait()

    @pl.when(j == 0)
    def _():
        m_sc[...] = jnp.full(m_sc.shape, NEG_BIG, F32)
        l_sc[...] = jnp.zeros(l_sc.shape, F32)
        acc_sc[...] = jnp.zeros(acc_sc.shape, F32)

    ql = qlat_ref[0]
    qr = qrope_ref[0]
    per_chain = g_pages // DECODE_CHAINS
    chains = []
    for c in range(DECODE_CHAINS):
        lats = []
        ss = []
        for g in range(c * per_chain, (c + 1) * per_chain):
            lat = lat_buf[slot, g].astype(BF16)
            rpt = rope_buf[slot, g].astype(BF16)
            lats.append(lat)
            ss.append(_dot_nt(ql, lat) + _dot(qr, rpt))
        chains.append((jnp.concatenate(ss, axis=1), lats))
    m_run = m_sc[...]
    l_run = l_sc[...]
    acc = acc_sc[...]
    for s, lats in chains:
        m_next = jnp.maximum(m_run, jnp.max(s, axis=1, keepdims=True))
        alpha = jnp.exp2(m_run - m_next)
        p = jnp.exp2(s - m_next[:, 0:1])
        l_run = alpha * l_run + jnp.sum(p, axis=1, keepdims=True)
        pb = p.astype(BF16)
        pv = _dot(pb[:, 0:PAGE_SIZE], lats[0])
        for g in range(1, per_chain):
            pv = pv + _dot(pb[:, g * PAGE_SIZE:(g + 1) * PAGE_SIZE], lats[g])
        acc = acc * alpha[:, 0:1] + pv
        m_run = m_next
    m_sc[...] = m_run
    l_sc[...] = l_run
    acc_sc[...] = acc

    @pl.when(j == nj - 1)
    def _():
        qlf = ql.astype(F32)
        qrf = qr.astype(F32)
        s_new = (jnp.sum(qlf * latn_ref[0], axis=1, keepdims=True)
                 + jnp.sum(qrf * ropen_ref[0], axis=1, keepdims=True))
        m_old = m_sc[...][:, 0:1]
        m_fin = jnp.maximum(m_old, s_new)
        a_old = jnp.exp2(m_old - m_fin)
        p_new = jnp.exp2(s_new - m_fin)
        l_fin = a_old * l_sc[...][:, 0:1] + p_new
        o_ref[0] = (acc_sc[...] * a_old + p_new * latn_ref[0]) / l_fin


def _decode(page_table, qlat, qrope, lat_new, rope_new, cache_lat, cache_rope):
    b = qlat.shape[0]
    n_pages = page_table.shape[1]
    g_pages = PAGES_PER_STEP
    in_specs = [
        pl.BlockSpec((1, N_HEADS, KV_RANK), lambda s, j, pt: (s, 0, 0)),
        pl.BlockSpec((1, N_HEADS, QK_ROPE), lambda s, j, pt: (s, 0, 0)),
        pl.BlockSpec((1, 1, KV_RANK), lambda s, j, pt: (s, 0, 0)),
        pl.BlockSpec((1, 1, QK_ROPE), lambda s, j, pt: (s, 0, 0)),
        pl.BlockSpec(memory_space=pl.ANY),
        pl.BlockSpec(memory_space=pl.ANY),
    ]
    assert n_pages % g_pages == 0
    grid_spec = pltpu.PrefetchScalarGridSpec(
        num_scalar_prefetch=1,
        grid=(b, n_pages // g_pages),
        in_specs=in_specs,
        out_specs=pl.BlockSpec((1, N_HEADS, KV_RANK), lambda s, j, pt: (s, 0, 0)),
        scratch_shapes=[
            pltpu.VMEM((2, g_pages, PAGE_SIZE, KV_RANK), F32),
            pltpu.VMEM((2, g_pages, QK_ROPE, PAGE_SIZE), F32),
            pltpu.SemaphoreType.DMA((2, 2)),
            pltpu.VMEM((N_HEADS, LANES), F32),
            pltpu.VMEM((N_HEADS, LANES), F32),
            pltpu.VMEM((N_HEADS, KV_RANK), F32),
        ],
    )
    return pl.pallas_call(
        functools.partial(_decode_kernel, n_seq=b, nj=n_pages // g_pages), grid_spec=grid_spec,
        out_shape=jax.ShapeDtypeStruct((b, N_HEADS, KV_RANK), F32),
        compiler_params=pltpu.CompilerParams(dimension_semantics=("arbitrary", "arbitrary"),
                                             vmem_limit_bytes=VMEM_LIMIT),
        name="sample_decode",
    )(page_table, qlat, qrope, lat_new, rope_new, cache_lat, cache_rope)


def _spost_kernel(x_ref, olat_ref, cvn_ref, mod_ref, wuvbd_ref, ona_ref, wout_ref, norm2_ref, wup_ref,
                  fcw_ref, fcb_ref, wdn_ref, fn_ref, st_ref, y_out, st_out):
    mod = mod_ref[...]
    g1 = mod[:, 2 * D_MODEL:3 * D_MODEL]
    sh2 = mod[:, 3 * D_MODEL:4 * D_MODEL]
    sc2 = mod[:, 4 * D_MODEL:5 * D_MODEL]
    g2 = mod[:, 5 * D_MODEL:6 * D_MODEL]
    attn = _dot(olat_ref[...].astype(BF16), wuvbd_ref[...])
    an = _rms(attn, ona_ref[...])
    x1 = _mix_residual(x_ref[...], an.astype(BF16), cvn_ref[...], g1, wout_ref)
    h2 = (_rms(x1, norm2_ref[...]) * (1.0 + sc2) + sh2).astype(BF16)
    u = _dot(h2, wup_ref[...])
    w = 2 * D_FF
    up = fcb_ref[...] + fcw_ref[0:1, :] * st_ref[:, 0:w] + fcw_ref[1:2, :] * st_ref[:, w:2 * w] + fcw_ref[2:3, :] * u
    st_out[:, 0:w] = st_ref[:, w:2 * w]
    st_out[:, w:2 * w] = u
    act = (_silu(up[:, 0:D_FF]) * up[:, D_FF:w]).astype(BF16)
    x2 = x1 + g2 * _dot(act, wdn_ref[...])
    y_out[...] = _rms(x2, fn_ref[...])


def _sample_post(x, olat, cvn, mod_s, wuvbd, ona, wout, norm2, wup, fcw, fcb, wdn, fnorm, st):
    b = x.shape[0]
    out_shape = (jax.ShapeDtypeStruct((b, D_MODEL), F32), jax.ShapeDtypeStruct(st.shape, F32))
    return pl.pallas_call(
        _spost_kernel, out_shape=out_shape,
        compiler_params=pltpu.CompilerParams(vmem_limit_bytes=VMEM_LIMIT),
        name="sample_post",
    )(x, olat, cvn, mod_s, wuvbd, ona, wout, norm2, wup, fcw, fcb, wdn, fnorm, st)


def _rope_tables(pos):
    freqs = ROPE_THETA ** (-jnp.arange(ROPE_HALF, dtype=F32) / ROPE_HALF)
    ang = pos.astype(F32)[:, None] * freqs[None, :]
    cos = jnp.cos(ang)
    sin = jnp.sin(ang)
    s = pos.shape[0]
    c2 = jnp.concatenate([cos, cos], axis=1)
    s2 = jnp.concatenate([sin, sin], axis=1)
    zpad = jnp.zeros((s, HEAD_PAD - QK_ROPE), F32)
    qscale = ATTN_SCALE * LOG2E
    cosq = jnp.concatenate([c2, jnp.ones((s, QK_NOPE), F32), jnp.zeros((s, HEAD_PAD - QK_ROPE - QK_NOPE), F32)],
                           axis=1) * qscale
    sinq = jnp.concatenate([s2, zpad], axis=1) * qscale
    cosk = jnp.concatenate([c2, zpad], axis=1)
    sink = jnp.concatenate([s2, zpad], axis=1)
    return jnp.stack([cosq, sinq, cosk, sink], axis=0)


def _rot_half_cols(w):
    return jnp.concatenate([-w[..., ROPE_HALF:], w[..., :ROPE_HALF]], axis=-1)


def _pack_weights(w_in, w_uq, w_uk, w_uv):
    o1, o2, o3 = Q_RANK, Q_RANK + KV_RANK, Q_RANK + KV_RANK + QK_ROPE
    wkr = w_in[:, o2:o3]
    zpad = jnp.zeros((D_MODEL, HEAD_PAD - QK_ROPE), F32)
    wz = jnp.concatenate([w_in[:, :o2], w_in[:, o3:], wkr, zpad, _rot_half_cols(wkr), zpad], axis=1).astype(BF16)

    wq = w_uq.reshape(Q_RANK, N_HEADS, QK_NOPE + QK_ROPE)
    wq_nope, wq_rope = wq[..., :QK_NOPE], wq[..., QK_NOPE:]
    ztail = jnp.zeros((Q_RANK, N_HEADS, HEAD_PAD - QK_ROPE - QK_NOPE), F32)
    wq_plain = jnp.concatenate([wq_rope, wq_nope, ztail], axis=2).reshape(Q_RANK, N_HEADS * HEAD_PAD)
    wq_rot = jnp.concatenate([_rot_half_cols(wq_rope), jnp.zeros((Q_RANK, N_HEADS, HEAD_PAD - QK_ROPE), F32)],
                             axis=2).reshape(Q_RANK, N_HEADS * HEAD_PAD)
    wq2 = jnp.concatenate([wq_plain, wq_rot], axis=1).astype(BF16)

    wuk = jnp.concatenate([jnp.zeros((KV_RANK, N_HEADS, QK_ROPE), F32), w_uk,
                           jnp.zeros((KV_RANK, N_HEADS, HEAD_PAD - QK_ROPE - QK_NOPE), F32)],
                          axis=2).reshape(KV_RANK, N_HEADS * HEAD_PAD).astype(BF16)
    wuvt = jnp.transpose(w_uv, (1, 2, 0))
    wuvt = jnp.concatenate([wuvt, jnp.zeros((N_HEADS, V_EXT - V_DIM, KV_RANK), F32)],
                           axis=1).reshape(N_HEADS * V_EXT, KV_RANK).astype(BF16)

    wukt = jnp.transpose(w_uk, (1, 2, 0))
    wukt = jnp.concatenate([jnp.zeros((N_HEADS, QK_ROPE, KV_RANK), F32), wukt,
                            jnp.zeros((N_HEADS, HEAD_PAD - QK_ROPE - QK_NOPE, KV_RANK), F32)], axis=1)
    eye = jnp.eye(N_HEADS, dtype=F32)
    wukt_bd = (eye[:, None, :, None] * wukt[:, :, None, :]).reshape(N_HEADS * HEAD_PAD, N_HEADS * KV_RANK).astype(BF16)
    wuvh = jnp.transpose(w_uv, (1, 0, 2))
    wuv_bd = (eye[:, None, :, None] * wuvh[:, :, None, :]).reshape(N_HEADS * KV_RANK, ATTN_WIDTH).astype(BF16)
    return wz, wq2, wuk, wuvt, wukt_bd, wuv_bd


def kernel(x_prompt, x_sample, c_prompt, c_sample, cache_kv_latent, cache_k_rope, state_conv, state_ffn_conv, page_table, w_ada, b_ada, norm1, w_in, q_norm, w_uq, kv_norm, w_uk, w_uv, conv_w, conv_b, conv_ln_g, conv_ln_b, out_norm_attn, out_norm_conv, w_out, norm2, w_up, ffn_conv_w, ffn_conv_b, w_down, final_norm):
    assert w_ada.shape[0] == 1, "single-layer decoder"
    n_p, s_p, _ = x_prompt.shape
    n_s, s_s, _ = x_sample.shape
    assert s_s == 1

    row = lambda a: a.reshape(1, -1)
    wz, wq2, wuk, wuvt, wukt_bd, wuv_bd = _pack_weights(w_in[0], w_uq[0], w_uk[0], w_uv[0])
    wout = w_out[0].astype(BF16)
    wup = w_up[0].astype(BF16)
    wdn = w_down[0].astype(BF16)
    norm1_r, qn_r, kvn_r = row(norm1[0]), row(q_norm[0]), row(kv_norm[0])
    cw, cb = conv_w[0], row(conv_b[0])
    lng, lnb, onc, ona = row(conv_ln_g[0]), row(conv_ln_b[0]), row(out_norm_conv[0]), row(out_norm_attn[0])
    norm2_r, fcw, fcb, fnorm = row(norm2[0]), ffn_conv_w[0], row(ffn_conv_b[0]), row(final_norm)

    n_c = n_p + n_s
    n_c_pad = -(-n_c // SUBLANES) * SUBLANES
    c_all = jnp.concatenate([c_prompt, c_sample, jnp.zeros((n_c_pad - n_c, D_MODEL), F32)], axis=0)
    mod = _modulation(c_all, w_ada[0], b_ada[0])
    mod_p = mod[:n_p].reshape(n_p, N_MOD, D_MODEL)
    mod_s = mod[n_p:n_c]

    tab_p = _rope_tables(jnp.arange(s_p))
    tabq_t = jnp.transpose(tab_p[0:2], (0, 2, 1))
    qt, k, vt, kv_lat_p, k_rope_p, cvn_p, conv_st_p = _prompt_pre(
        x_prompt, mod_p, tabq_t, tab_p[2:4], norm1_r, wz, qn_r, wq2.T, kvn_r, wuk, wuvt, cw, cb, lng, lnb, onc)
    attn_p = _flash(qt, k, vt)
    y_p, ffn_st_p = _prompt_post(x_prompt, attn_p, cvn_p, mod_p, ona, wout, norm2_r, wup, fcw, fcb, wdn, fnorm)

    tab_s = _rope_tables(PAST_LEN + jnp.arange(s_s))
    st_conv = state_conv[0].reshape(n_s, (CONV_K - 1) * CONV_WIDTH)
    qlat, q_s, lat_s, kr_s, cvn_s, st_conv_new = _sample_pre(
        x_sample[:, 0], mod_s, tab_s, norm1_r, wz, qn_r, wq2, kvn_r, wukt_bd, cw, cb, lng, lnb, onc, st_conv)
    qrope = q_s.reshape(n_s, N_HEADS, HEAD_PAD)[:, :, :QK_ROPE]
    o_lat = _decode(page_table, qlat.reshape(n_s, N_HEADS, KV_RANK), qrope,
                    lat_s.reshape(n_s, 1, KV_RANK), kr_s.reshape(n_s, 1, QK_ROPE),
                    cache_kv_latent, jnp.swapaxes(cache_k_rope, 2, 3))
    st_ffn = state_ffn_conv[0].reshape(n_s, (FFN_K - 1) * 2 * D_FF)
    y_s, st_ffn_new = _sample_post(
        x_sample[:, 0], o_lat.reshape(n_s, N_HEADS * KV_RANK), cvn_s, mod_s, wuv_bd, ona, wout, norm2_r,
        wup, fcw, fcb, wdn, fnorm, st_ffn)

    return (
        y_p,
        y_s.reshape(n_s, 1, D_MODEL),
        kv_lat_p,
        k_rope_p,
        conv_st_p,
        ffn_st_p,
        lat_s.reshape(1, n_s, 1, KV_RANK),
        kr_s.reshape(1, n_s, 1, QK_ROPE),
        st_conv_new.reshape(1, n_s, CONV_K - 1, CONV_WIDTH),
        st_ffn_new.reshape(1, n_s, FFN_K - 1, 2 * D_FF),
    )
```

```python
import functools
import math

import jax
import jax.numpy as jnp
from jax import lax
from jax.experimental import pallas as pl
from jax.experimental.pallas import tpu as pltpu

D_MODEL = 1024
N_HEADS = 8
QK_NOPE = 64
QK_ROPE = 32
V_DIM = 64
Q_RANK = 384
KV_RANK = 256
ATTN_WIDTH = N_HEADS * V_DIM
CONV_WIDTH = 512
CONV_K = 31
D_FF = 2816
FFN_K = 3
N_MOD = 6
ROPE_THETA = 10000.0
EPS = 1e-6
PAST_LEN = 16384
PAGE_SIZE = 128
ATTN_SCALE = (QK_NOPE + QK_ROPE) ** -0.5
LOG2E = math.log2(math.e)

LANES = 128
SUBLANES = 8
HEAD_PAD = LANES
V_EXT = V_DIM + 16
ROPE_HALF = QK_ROPE // 2
NEG_BIG = -1e30

TM_PRE = 512
TM_POST = 512
FF_CHUNK = 2816
TQ = 2048
TK = 512
FLASH_LOOKAHEAD = 2
PAGES_PER_STEP = 128
DECODE_CHAINS = 4
CARRY_ROWS = 32
VMEM_LIMIT = 56 * 1024 * 1024

Z_CQ = 0
Z_CKV = Z_CQ + Q_RANK
Z_GA = Z_CKV + KV_RANK
Z_GB = Z_GA + CONV_WIDTH
Z_KR = Z_GB + CONV_WIDTH
Z_KRR = Z_KR + HEAD_PAD
Z_END = Z_KRR + HEAD_PAD

BF16 = jnp.bfloat16
F32 = jnp.float32


def _dot(a, b):
    return jnp.dot(a, b, preferred_element_type=F32)


def _dot_nt(a, b):
    return lax.dot_general(a, b, (((1,), (1,)), ((), ())), preferred_element_type=F32)


def _rms(x, g):
    return x * lax.rsqrt(jnp.mean(x * x, axis=-1, keepdims=True) + EPS) * g


def _silu(x):
    return x * jax.nn.sigmoid(x)


def _mod_kernel(c_ref, w_ref, b_ref, o_ref):
    s = _silu(c_ref[...])
    o_ref[...] = _dot(s.astype(BF16), w_ref[...].astype(BF16)) + b_ref[...]


def _modulation(c, w_ada, b_ada):
    n = c.shape[0]
    tn = 1024
    return pl.pallas_call(
        _mod_kernel,
        grid=(N_MOD * D_MODEL // tn,),
        in_specs=[
            pl.BlockSpec((n, D_MODEL), lambda j: (0, 0)),
            pl.BlockSpec((D_MODEL, tn), lambda j: (0, j)),
            pl.BlockSpec((1, tn), lambda j: (0, j)),
        ],
        out_specs=pl.BlockSpec((n, tn), lambda j: (0, j)),
        out_shape=jax.ShapeDtypeStruct((n, N_MOD * D_MODEL), F32),
        compiler_params=pltpu.CompilerParams(dimension_semantics=("arbitrary",), vmem_limit_bytes=VMEM_LIMIT),
        name="adaln_mod",
    )(c, w_ada, b_ada.reshape(1, -1))


def _front(x, sh1, sc1, norm1, wz_ref, qn, kvn, cosk, sink):
    h = _rms(x, norm1) * (1.0 + sc1) + sh1
    z = _dot(h.astype(BF16), wz_ref[...])
    cqn = _rms(z[:, Z_CQ:Z_CKV], qn).astype(BF16)
    lat = _rms(z[:, Z_CKV:Z_GA], kvn)
    kro = z[:, Z_KR:Z_KRR] * cosk + z[:, Z_KRR:Z_END] * sink
    glu = z[:, Z_GA:Z_GB] * jax.nn.sigmoid(z[:, Z_GB:Z_KR])
    return cqn, lat, kro, glu


def _conv_tail(cv, lng, lnb, onc):
    mu = jnp.mean(cv, axis=-1, keepdims=True)
    d = cv - mu
    y = d * lax.rsqrt(jnp.mean(d * d, axis=-1, keepdims=True) + EPS) * lng + lnb
    return _rms(_silu(y), onc)


def _pre_kernel(x_ref, mod_ref, tabq_ref, tabk_ref, norm1_ref, wz_ref, qn_ref, wq2t_ref, kvn_ref, wuk_ref, wuvt_ref,
                cw_ref, cb_ref, lng_ref, lnb_ref, onc_ref,
                qt_out, k_out, vt_out, lat_out, kr_out, cvn_out, cst_out,
                ext_ref, shift_ref):
    i = pl.program_id(1)
    tm = x_ref.shape[1]
    m = mod_ref[0]
    cqn, lat, kro, glu = _front(x_ref[0], m[0:1], m[1:2], norm1_ref[...], wz_ref, qn_ref[...],
                                kvn_ref[...], tabk_ref[0], tabk_ref[1])
    q2t = _dot_nt(wq2t_ref[...], cqn)
    cosqt = tabq_ref[0]
    sinqt = tabq_ref[1]
    hw = N_HEADS * HEAD_PAD
    for hd in range(N_HEADS):
        lo = hd * HEAD_PAD
        qt_out[0, 0, lo:lo + HEAD_PAD, :] = (q2t[lo:lo + HEAD_PAD, :] * cosqt
                                             + q2t[hw + lo:hw + lo + HEAD_PAD, :] * sinqt).astype(BF16)
    lat_out[0, 0] = lat
    kr_out[0, 0] = kro[:, :QK_ROPE]
    latb = lat.astype(BF16)
    kn = _dot(latb, wuk_ref[...])
    for hd in range(N_HEADS):
        k_out[0, :, hd * HEAD_PAD:(hd + 1) * HEAD_PAD] = (kn[:, hd * HEAD_PAD:(hd + 1) * HEAD_PAD] + kro).astype(BF16)
    vt = _dot_nt(wuvt_ref[...], latb)
    vrow = lax.broadcasted_iota(jnp.int32, vt.shape, 0)
    is_one = functools.reduce(jnp.logical_or, [vrow == hd * V_EXT + V_DIM for hd in range(N_HEADS)])
    vt_out[0, 0] = jnp.where(is_one, 1.0, vt).astype(BF16)

    @pl.when(i == 0)
    def _():
        ext_ref[0:CARRY_ROWS, :] = jnp.zeros((CARRY_ROWS, CONV_WIDTH), F32)

    ext_ref[CARRY_ROWS:CARRY_ROWS + tm, :] = glu
    base = CARRY_ROWS - (CONV_K - 1)
    cv = jnp.broadcast_to(cb_ref[...], (tm, CONV_WIDTH))
    for r in range(SUBLANES):
        taps = [k for k in range(CONV_K) if (base + k) % SUBLANES == r]
        if not taps:
            continue
        span = max((base + k) // SUBLANES for k in taps) * SUBLANES
        if r:
            shift_ref[0:tm + span, :] = ext_ref[pl.ds(r, tm + span), :]
        src = shift_ref if r else ext_ref
        for k in taps:
            a = (base + k) // SUBLANES * SUBLANES
            cv = cv + cw_ref[k:k + 1, :] * src[pl.ds(a, tm), :]
    ext_ref[0:CARRY_ROWS, :] = ext_ref[tm:tm + CARRY_ROWS, :]

    @pl.when(i == pl.num_programs(1) - 1)
    def _():
        cst_out[0, 0] = ext_ref[base:CARRY_ROWS, :]

    cvn_out[0] = _conv_tail(cv, lng_ref[...], lnb_ref[...], onc_ref[...]).astype(BF16)


def _const_spec(shape):
    nd = len(shape)
    return pl.BlockSpec(shape, lambda *_: (0,) * nd, pipeline_mode=pl.Buffered(1))


def _prompt_pre(x, mod_p, tabq, tabk, norm1, wz, qn, wq2t, kvn, wuk, wuvt, cw, cb, lng, lnb, onc):
    n, s, _ = x.shape
    tm = TM_PRE
    grid = (n, s // tm)
    tok = lambda w: pl.BlockSpec((1, tm, w), lambda b, i: (b, i, 0))
    tok_t = lambda w: pl.BlockSpec((1, 1, w, tm), lambda b, i: (b, i, 0, 0))
    in_specs = [
        tok(D_MODEL),
        pl.BlockSpec((1, N_MOD, D_MODEL), lambda b, i: (b, 0, 0)),
        pl.BlockSpec((2, HEAD_PAD, tm), lambda b, i: (0, 0, i)),
        pl.BlockSpec((2, tm, LANES), lambda b, i: (0, i, 0)),
        _const_spec(norm1.shape), _const_spec(wz.shape), _const_spec(qn.shape), _const_spec(wq2t.shape),
        _const_spec(kvn.shape), _const_spec(wuk.shape), _const_spec(wuvt.shape),
        _const_spec(cw.shape), _const_spec(cb.shape), _const_spec(lng.shape), _const_spec(lnb.shape),
        _const_spec(onc.shape),
    ]
    out_shape = (
        jax.ShapeDtypeStruct((n, s // tm, N_HEADS * HEAD_PAD, tm), BF16),
        jax.ShapeDtypeStruct((n, s, N_HEADS * HEAD_PAD), BF16),
        jax.ShapeDtypeStruct((n, s // tm, N_HEADS * V_EXT, tm), BF16),
        jax.ShapeDtypeStruct((1, n, s, KV_RANK), F32),
        jax.ShapeDtypeStruct((1, n, s, QK_ROPE), F32),
        jax.ShapeDtypeStruct((n, s, CONV_WIDTH), BF16),
        jax.ShapeDtypeStruct((1, n, CONV_K - 1, CONV_WIDTH), F32),
    )
    out_specs = (
        tok_t(N_HEADS * HEAD_PAD), tok(N_HEADS * HEAD_PAD), tok_t(N_HEADS * V_EXT),
        pl.BlockSpec((1, 1, tm, KV_RANK), lambda b, i: (0, b, i, 0)),
        pl.BlockSpec((1, 1, tm, QK_ROPE), lambda b, i: (0, b, i, 0)),
        tok(CONV_WIDTH),
        pl.BlockSpec((1, 1, CONV_K - 1, CONV_WIDTH), lambda b, i: (0, b, 0, 0)),
    )
    return pl.pallas_call(
        _pre_kernel, grid=grid, in_specs=in_specs, out_specs=out_specs, out_shape=out_shape,
        scratch_shapes=[pltpu.VMEM((CARRY_ROWS + tm, CONV_WIDTH), F32),
                        pltpu.VMEM((CARRY_ROWS + tm, CONV_WIDTH), F32)],
        compiler_params=pltpu.CompilerParams(dimension_semantics=("arbitrary", "arbitrary"),
                                             vmem_limit_bytes=VMEM_LIMIT),
        name="prompt_pre",
    )(x, mod_p, tabq, tabk, norm1, wz, qn, wq2t, kvn, wuk, wuvt, cw, cb, lng, lnb, onc)


def _flash_kernel(qi_ref, ki_ref, qt_ref, k_ref, vt_ref, o_ref, m_sc, acc_sc):
    t = pl.program_id(1)
    qi = qi_ref[t]
    ki = ki_ref[t]
    n_groups, wq = qt_ref.shape[1], qt_ref.shape[3]
    tq = n_groups * wq
    tk = k_ref.shape[1]

    @pl.when(ki == 0)
    def _():
        m_sc[...] = jnp.full(m_sc.shape, NEG_BIG, F32)
        acc_sc[...] = jnp.zeros(acc_sc.shape, F32)

    units = [(hd, c * wq) for hd in range(N_HEADS) for c in range(n_groups)]

    def scores(unit):
        hd, q0 = unit
        qt = qt_ref[0, q0 // wq, hd * HEAD_PAD:(hd + 1) * HEAD_PAD, :]
        k = k_ref[0, :, hd * HEAD_PAD:(hd + 1) * HEAD_PAD]
        return _dot(k, qt)

    def step(key0):
        live = units
        if key0 is not None:
            key = lax.broadcasted_iota(jnp.int32, (tk, wq), 0)
            qry = lax.broadcasted_iota(jnp.int32, (tk, wq), 1)
            live = [(hd, q0) for hd, q0 in units if q0 + wq > key0]
        pending = [scores(u) for u in live[:FLASH_LOOKAHEAD]]
        for i, (hd, q0) in enumerate(live):
            st = pending.pop(0)
            if i + FLASH_LOOKAHEAD < len(live):
                pending.append(scores(live[i + FLASH_LOOKAHEAD]))
            if key0 is not None and q0 < key0 + tk - 1:
                st = jnp.where(key + key0 <= qry + q0, st, NEG_BIG)
            vt = vt_ref[0, 0, hd * V_EXT:(hd + 1) * V_EXT, :]
            m_prev = m_sc[hd, :, q0:q0 + wq]
            m_next = jnp.maximum(m_prev, jnp.max(st, axis=0, keepdims=True))
            alpha = jnp.exp2(m_prev - m_next)
            p = jnp.exp2(st - m_next)
            acc_sc[hd, :, q0:q0 + wq] = acc_sc[hd, :, q0:q0 + wq] * alpha + _dot(vt, p.astype(BF16))
            m_sc[hd, :, q0:q0 + wq] = m_next

    ratio = tq // tk

    @pl.when(ki < qi * ratio)
    def _():
        step(None)

    for d in range(ratio):
        @pl.when(ki == qi * ratio + d)
        def _(d=d):
            step(d * tk)

    @pl.when(ki == qi * ratio + ratio - 1)
    def _():
        for pr in range(N_HEADS // 2):
            halves = []
            for hd in (2 * pr, 2 * pr + 1):
                inv = 1.0 / acc_sc[hd, V_DIM:V_DIM + 1, :]
                halves.append(acc_sc[hd, 0:V_DIM, :] * inv)
            o_ref[0, :, pr * LANES:(pr + 1) * LANES] = jnp.concatenate(halves, axis=0).T


def _flash(qt, k, vt):
    n, s, _ = k.shape
    tq, tk = TQ, TK
    tile = qt.shape[3]
    assert tq % tk == 0 and s % tq == 0 and tq % tile == 0 and tk == tile
    pairs = [(a, b) for a in range(s // tq) for b in range((a + 1) * (tq // tk))]
    qi = jnp.asarray([p[0] for p in pairs], jnp.int32)
    ki = jnp.asarray([p[1] for p in pairs], jnp.int32)
    grid_spec = pltpu.PrefetchScalarGridSpec(
        num_scalar_prefetch=2,
        grid=(n, len(pairs)),
        in_specs=[
            pl.BlockSpec((1, tq // tile, N_HEADS * HEAD_PAD, tile), lambda b, t, qi, ki: (b, qi[t], 0, 0)),
            pl.BlockSpec((1, tk, N_HEADS * HEAD_PAD), lambda b, t, qi, ki: (b, ki[t], 0)),
            pl.BlockSpec((1, 1, N_HEADS * V_EXT, tk), lambda b, t, qi, ki: (b, ki[t], 0, 0)),
        ],
        out_specs=pl.BlockSpec((1, tq, ATTN_WIDTH), lambda b, t, qi, ki: (b, qi[t], 0)),
        scratch_shapes=[
            pltpu.VMEM((N_HEADS, 1, tq), F32),
            pltpu.VMEM((N_HEADS, V_EXT, tq), F32),
        ],
    )
    return pl.pallas_call(
        _flash_kernel, grid_spec=grid_spec,
        out_shape=jax.ShapeDtypeStruct((n, s, ATTN_WIDTH), F32),
        compiler_params=pltpu.CompilerParams(dimension_semantics=("arbitrary", "arbitrary"),
                                             vmem_limit_bytes=VMEM_LIMIT),
        name="prompt_flash",
    )(qi, ki, qt, k, vt)


def _mix_residual(x, an_b, cvn_b, g1, wout_ref):
    mix = _dot(an_b, wout_ref[0:ATTN_WIDTH, :]) + _dot(cvn_b, wout_ref[ATTN_WIDTH:, :])
    return x + g1 * mix


def _post_kernel(x_ref, attn_ref, cvn_ref, mod_ref, ona_ref, wout_ref, norm2_ref, wup_ref, fcw_ref, fcb_ref,
                 wdn_ref, fn_ref, y_out, fst_out, ext_ref, carry_ref):
    i = pl.program_id(1)
    tm = x_ref.shape[1]
    m = mod_ref[0]
    g1, sh2, sc2, g2 = m[2:3], m[3:4], m[4:5], m[5:6]
    an = _rms(attn_ref[0], ona_ref[...])
    x1 = _mix_residual(x_ref[0], an.astype(BF16), cvn_ref[0], g1, wout_ref)
    h2 = (_rms(x1, norm2_ref[...]) * (1.0 + sc2) + sh2).astype(BF16)

    @pl.when(i == 0)
    def _():
        carry_ref[...] = jnp.zeros(carry_ref.shape, F32)

    def conv_part(off):
        u = _dot(h2, wup_ref[:, off:off + FF_CHUNK])
        ext_ref[0:SUBLANES, :] = carry_ref[:, off:off + FF_CHUNK]
        ext_ref[SUBLANES:SUBLANES + tm, :] = u
        carry_ref[:, off:off + FF_CHUNK] = ext_ref[tm:tm + SUBLANES, :]
        up = fcb_ref[:, off:off + FF_CHUNK] + fcw_ref[2:3, off:off + FF_CHUNK] * u
        for k in range(FFN_K - 1):
            up = up + fcw_ref[k:k + 1, off:off + FF_CHUNK] * ext_ref[pl.ds(SUBLANES - (FFN_K - 1) + k, tm), :]
        return up

    ffn = jnp.zeros((tm, D_MODEL), F32)
    for c in range(D_FF // FF_CHUNK):
        a = conv_part(c * FF_CHUNK)
        v = conv_part(D_FF + c * FF_CHUNK)
        act = (_silu(a) * v).astype(BF16)
        ffn = ffn + _dot(act, wdn_ref[c * FF_CHUNK:(c + 1) * FF_CHUNK, :])
    x2 = x1 + g2 * ffn
    y_out[0] = _rms(x2, fn_ref[...])

    @pl.when(i == pl.num_programs(1) - 1)
    def _():
        fst_out[0, 0] = carry_ref[SUBLANES - (FFN_K - 1):SUBLANES, :]


def _prompt_post(x, attn, cvn, mod_p, ona, wout, norm2, wup, fcw, fcb, wdn, fnorm):
    n, s, _ = x.shape
    tm = TM_POST
    tok = lambda w: pl.BlockSpec((1, tm, w), lambda b, i: (b, i, 0))
    in_specs = [
        tok(D_MODEL), tok(ATTN_WIDTH), tok(CONV_WIDTH),
        pl.BlockSpec((1, N_MOD, D_MODEL), lambda b, i: (b, 0, 0)),
        _const_spec(ona.shape), _const_spec(wout.shape), _const_spec(norm2.shape), _const_spec(wup.shape),
        _const_spec(fcw.shape), _const_spec(fcb.shape), _const_spec(wdn.shape), _const_spec(fnorm.shape),
    ]
    out_shape = (
        jax.ShapeDtypeStruct((n, s, D_MODEL), F32),
        jax.ShapeDtypeStruct((1, n, FFN_K - 1, 2 * D_FF), F32),
    )
    out_specs = (
        tok(D_MODEL),
        pl.BlockSpec((1, 1, FFN_K - 1, 2 * D_FF), lambda b, i: (0, b, 0, 0)),
    )
    return pl.pallas_call(
        _post_kernel, grid=(n, s // tm), in_specs=in_specs, out_specs=out_specs, out_shape=out_shape,
        scratch_shapes=[pltpu.VMEM((SUBLANES + tm, FF_CHUNK), F32), pltpu.VMEM((SUBLANES, 2 * D_FF), F32)],
        compiler_params=pltpu.CompilerParams(dimension_semantics=("arbitrary", "arbitrary"),
                                             vmem_limit_bytes=VMEM_LIMIT),
        name="prompt_post",
    )(x, attn, cvn, mod_p, ona, wout, norm2, wup, fcw, fcb, wdn, fnorm)


def _spre_kernel(x_ref, mod_ref, tab_ref, norm1_ref, wz_ref, qn_ref, wq2_ref, kvn_ref, wukt_ref,
                 cw_ref, cb_ref, lng_ref, lnb_ref, onc_ref, st_ref,
                 qlat_out, q_out, lat_out, kr_out, cvn_out, st_out):
    mod = mod_ref[...]
    sh1 = mod[:, 0:D_MODEL]
    sc1 = mod[:, D_MODEL:2 * D_MODEL]
    cqn, lat, kro, glu = _front(x_ref[...], sh1, sc1, norm1_ref[...], wz_ref, qn_ref[...],
                                kvn_ref[...], tab_ref[2], tab_ref[3])
    q2 = _dot(cqn, wq2_ref[...])
    qs = []
    for hd in range(N_HEADS):
        a = q2[:, hd * HEAD_PAD:(hd + 1) * HEAD_PAD]
        b = q2[:, (N_HEADS + hd) * HEAD_PAD:(N_HEADS + hd + 1) * HEAD_PAD]
        qs.append(a * tab_ref[0] + b * tab_ref[1])
    q = jnp.concatenate(qs, axis=1).astype(BF16)
    q_out[...] = q
    qlat_out[...] = _dot(q, wukt_ref[...]).astype(BF16)
    lat_out[...] = lat
    kr_out[...] = kro[:, :QK_ROPE]
    kst = CONV_K - 1
    cv = cb_ref[...] + cw_ref[kst:kst + 1, :] * glu
    for k in range(kst):
        cv = cv + cw_ref[k:k + 1, :] * st_ref[:, k * CONV_WIDTH:(k + 1) * CONV_WIDTH]
    st_out[:, 0:(kst - 1) * CONV_WIDTH] = st_ref[:, CONV_WIDTH:kst * CONV_WIDTH]
    st_out[:, (kst - 1) * CONV_WIDTH:kst * CONV_WIDTH] = glu
    cvn_out[...] = _conv_tail(cv, lng_ref[...], lnb_ref[...], onc_ref[...]).astype(BF16)


def _sample_pre(x, mod_s, tab, norm1, wz, qn, wq2, kvn, wukt, cw, cb, lng, lnb, onc, st):
    b = x.shape[0]
    out_shape = (
        jax.ShapeDtypeStruct((b, N_HEADS * KV_RANK), BF16),
        jax.ShapeDtypeStruct((b, N_HEADS * HEAD_PAD), BF16),
        jax.ShapeDtypeStruct((b, KV_RANK), F32),
        jax.ShapeDtypeStruct((b, QK_ROPE), F32),
        jax.ShapeDtypeStruct((b, CONV_WIDTH), BF16),
        jax.ShapeDtypeStruct(st.shape, F32),
    )
    return pl.pallas_call(
        _spre_kernel, out_shape=out_shape,
        compiler_params=pltpu.CompilerParams(vmem_limit_bytes=VMEM_LIMIT),
        name="sample_pre",
    )(x, mod_s, tab, norm1, wz, qn, wq2, kvn, wukt, cw, cb, lng, lnb, onc, st)


def _page_copies(pt_ref, lat_hbm, rope_hbm, lat_buf, rope_buf, sem, step, slot, steps_per_seq):
    g_pages = PAGES_PER_STEP
    seq = lax.div(step, steps_per_seq)
    first = lax.rem(step, steps_per_seq) * g_pages
    copies = []
    for g in range(g_pages):
        page = pt_ref[seq, first + g]
        copies.append(pltpu.make_async_copy(lat_hbm.at[0, page], lat_buf.at[slot, g], sem.at[slot, 0]))
        copies.append(pltpu.make_async_copy(rope_hbm.at[0, page], rope_buf.at[slot, g], sem.at[slot, 1]))
    return copies


def _decode_kernel(pt_ref, qlat_ref, qrope_ref, latn_ref, ropen_ref, lat_hbm, rope_hbm, o_ref,
                   lat_buf, rope_buf, sem, m_sc, l_sc, acc_sc, *, n_seq, nj):
    g_pages = PAGES_PER_STEP
    j = pl.program_id(1)
    step = pl.program_id(0) * nj + j
    n_steps = n_seq * nj
    slot = lax.rem(step, 2)
    copies = functools.partial(_page_copies, pt_ref, lat_hbm, rope_hbm, lat_buf, rope_buf, sem,
                               steps_per_seq=nj)

    @pl.when(step == 0)
    def _():
        for c in copies(step=step, slot=slot):
            c.start()

    @pl.when(step + 1 < n_steps)
    def _():
        for c in copies(step=step + 1, slot=1 - slot):
            c.start()

    for c in copies(step=step, slot=slot):
        c.wait()

    @pl.when(j == 0)
    def _():
        m_sc[...] = jnp.full(m_sc.shape, NEG_BIG, F32)
        l_sc[...] = jnp.zeros(l_sc.shape, F32)
        acc_sc[...] = jnp.zeros(acc_sc.shape, F32)

    ql = qlat_ref[0]
    qr = qrope_ref[0]
    per_chain = g_pages // DECODE_CHAINS
    chains = []
    for c in range(DECODE_CHAINS):
        lats = []
        ss = []
        for g in range(c * per_chain, (c + 1) * per_chain):
            lat = lat_buf[slot, g].astype(BF16)
            rpt = rope_buf[slot, g].astype(BF16)
            lats.append(lat)
            ss.append(_dot_nt(ql, lat) + _dot(qr, rpt))
        chains.append((jnp.concatenate(ss, axis=1), lats))
    m_run = m_sc[...]
    l_run = l_sc[...]
    acc = acc_sc[...]
    for s, lats in chains:
        m_next = jnp.maximum(m_run, jnp.max(s, axis=1, keepdims=True))
        alpha = jnp.exp2(m_run - m_next)
        p = jnp.exp2(s - m_next[:, 0:1])
        l_run = alpha * l_run + jnp.sum(p, axis=1, keepdims=True)
        pb = p.astype(BF16)
        pv = _dot(pb[:, 0:PAGE_SIZE], lats[0])
        for g in range(1, per_chain):
            pv = pv + _dot(pb[:, g * PAGE_SIZE:(g + 1) * PAGE_SIZE], lats[g])
        acc = acc * alpha[:, 0:1] + pv
        m_run = m_next
    m_sc[...] = m_run
    l_sc[...] = l_run
    acc_sc[...] = acc

    @pl.when(j == nj - 1)
    def _():
        qlf = ql.astype(F32)
        qrf = qr.astype(F32)
        s_new = (jnp.sum(qlf * latn_ref[0], axis=1, keepdims=True)
                 + jnp.sum(qrf * ropen_ref[0], axis=1, keepdims=True))
        m_old = m_sc[...][:, 0:1]
        m_fin = jnp.maximum(m_old, s_new)
        a_old = jnp.exp2(m_old - m_fin)
        p_new = jnp.exp2(s_new - m_fin)
        l_fin = a_old * l_sc[...][:, 0:1] + p_new
        o_ref[0] = (acc_sc[...] * a_old + p_new * latn_ref[0]) / l_fin


def _decode(page_table, qlat, qrope, lat_new, rope_new, cache_lat, cache_rope):
    b = qlat.shape[0]
    n_pages = page_table.shape[1]
    g_pages = PAGES_PER_STEP
    in_specs = [
        pl.BlockSpec((1, N_HEADS, KV_RANK), lambda s, j, pt: (s, 0, 0)),
        pl.BlockSpec((1, N_HEADS, QK_ROPE), lambda s, j, pt: (s, 0, 0)),
        pl.BlockSpec((1, 1, KV_RANK), lambda s, j, pt: (s, 0, 0)),
        pl.BlockSpec((1, 1, QK_ROPE), lambda s, j, pt: (s, 0, 0)),
        pl.BlockSpec(memory_space=pl.ANY),
        pl.BlockSpec(memory_space=pl.ANY),
    ]
    assert n_pages % g_pages == 0
    grid_spec = pltpu.PrefetchScalarGridSpec(
        num_scalar_prefetch=1,
        grid=(b, n_pages // g_pages),
        in_specs=in_specs,
        out_specs=pl.BlockSpec((1, N_HEADS, KV_RANK), lambda s, j, pt: (s, 0, 0)),
        scratch_shapes=[
            pltpu.VMEM((2, g_pages, PAGE_SIZE, KV_RANK), F32),
            pltpu.VMEM((2, g_pages, QK_ROPE, PAGE_SIZE), F32),
            pltpu.SemaphoreType.DMA((2, 2)),
            pltpu.VMEM((N_HEADS, LANES), F32),
            pltpu.VMEM((N_HEADS, LANES), F32),
            pltpu.VMEM((N_HEADS, KV_RANK), F32),
        ],
    )
    return pl.pallas_call(
        functools.partial(_decode_kernel, n_seq=b, nj=n_pages // g_pages), grid_spec=grid_spec,
        out_shape=jax.ShapeDtypeStruct((b, N_HEADS, KV_RANK), F32),
        compiler_params=pltpu.CompilerParams(dimension_semantics=("arbitrary", "arbitrary"),
                                             vmem_limit_bytes=VMEM_LIMIT),
        name="sample_decode",
    )(page_table, qlat, qrope, lat_new, rope_new, cache_lat, cache_rope)


def _spost_kernel(x_ref, olat_ref, cvn_ref, mod_ref, wuvbd_ref, ona_ref, wout_ref, norm2_ref, wup_ref,
                  fcw_ref, fcb_ref, wdn_ref, fn_ref, st_ref, y_out, st_out):
    mod = mod_ref[...]
    g1 = mod[:, 2 * D_MODEL:3 * D_MODEL]
    sh2 = mod[:, 3 * D_MODEL:4 * D_MODEL]
    sc2 = mod[:, 4 * D_MODEL:5 * D_MODEL]
    g2 = mod[:, 5 * D_MODEL:6 * D_MODEL]
    attn = _dot(olat_ref[...].astype(BF16), wuvbd_ref[...])
    an = _rms(attn, ona_ref[...])
    x1 = _mix_residual(x_ref[...], an.astype(BF16), cvn_ref[...], g1, wout_ref)
    h2 = (_rms(x1, norm2_ref[...]) * (1.0 + sc2) + sh2).astype(BF16)
    u = _dot(h2, wup_ref[...])
    w = 2 * D_FF
    up = fcb_ref[...] + fcw_ref[0:1, :] * st_ref[:, 0:w] + fcw_ref[1:2, :] * st_ref[:, w:2 * w] + fcw_ref[2:3, :] * u
    st_out[:, 0:w] = st_ref[:, w:2 * w]
    st_out[:, w:2 * w] = u
    act = (_silu(up[:, 0:D_FF]) * up[:, D_FF:w]).astype(BF16)
    x2 = x1 + g2 * _dot(act, wdn_ref[...])
    y_out[...] = _rms(x2, fn_ref[...])


def _sample_post(x, olat, cvn, mod_s, wuvbd, ona, wout, norm2, wup, fcw, fcb, wdn, fnorm, st):
    b = x.shape[0]
    out_shape = (jax.ShapeDtypeStruct((b, D_MODEL), F32), jax.ShapeDtypeStruct(st.shape, F32))
    return pl.pallas_call(
        _spost_kernel, out_shape=out_shape,
        compiler_params=pltpu.CompilerParams(vmem_limit_bytes=VMEM_LIMIT),
        name="sample_post",
    )(x, olat, cvn, mod_s, wuvbd, ona, wout, norm2, wup, fcw, fcb, wdn, fnorm, st)


def _rope_tables(pos):
    freqs = ROPE_THETA ** (-jnp.arange(ROPE_HALF, dtype=F32) / ROPE_HALF)
    ang = pos.astype(F32)[:, None] * freqs[None, :]
    cos = jnp.cos(ang)
    sin = jnp.sin(ang)
    s = pos.shape[0]
    c2 = jnp.concatenate([cos, cos], axis=1)
    s2 = jnp.concatenate([sin, sin], axis=1)
    zpad = jnp.zeros((s, HEAD_PAD - QK_ROPE), F32)
    qscale = ATTN_SCALE * LOG2E
    cosq = jnp.concatenate([c2, jnp.ones((s, QK_NOPE), F32), jnp.zeros((s, HEAD_PAD - QK_ROPE - QK_NOPE), F32)],
                           axis=1) * qscale
    sinq = jnp.concatenate([s2, zpad], axis=1) * qscale
    cosk = jnp.concatenate([c2, zpad], axis=1)
    sink = jnp.concatenate([s2, zpad], axis=1)
    return jnp.stack([cosq, sinq, cosk, sink], axis=0)


def _rot_half_cols(w):
    return jnp.concatenate([-w[..., ROPE_HALF:], w[..., :ROPE_HALF]], axis=-1)


def _pack_weights(w_in, w_uq, w_uk, w_uv):
    o1, o2, o3 = Q_RANK, Q_RANK + KV_RANK, Q_RANK + KV_RANK + QK_ROPE
    wkr = w_in[:, o2:o3]
    zpad = jnp.zeros((D_MODEL, HEAD_PAD - QK_ROPE), F32)
    wz = jnp.concatenate([w_in[:, :o2], w_in[:, o3:], wkr, zpad, _rot_half_cols(wkr), zpad], axis=1).astype(BF16)

    wq = w_uq.reshape(Q_RANK, N_HEADS, QK_NOPE + QK_ROPE)
    wq_nope, wq_rope = wq[..., :QK_NOPE], wq[..., QK_NOPE:]
    ztail = jnp.zeros((Q_RANK, N_HEADS, HEAD_PAD - QK_ROPE - QK_NOPE), F32)
    wq_plain = jnp.concatenate([wq_rope, wq_nope, ztail], axis=2).reshape(Q_RANK, N_HEADS * HEAD_PAD)
    wq_rot = jnp.concatenate([_rot_half_cols(wq_rope), jnp.zeros((Q_RANK, N_HEADS, HEAD_PAD - QK_ROPE), F32)],
                             axis=2).reshape(Q_RANK, N_HEADS * HEAD_PAD)
    wq2 = jnp.concatenate([wq_plain, wq_rot], axis=1).astype(BF16)

    wuk = jnp.concatenate([jnp.zeros((KV_RANK, N_HEADS, QK_ROPE), F32), w_uk,
                           jnp.zeros((KV_RANK, N_HEADS, HEAD_PAD - QK_ROPE - QK_NOPE), F32)],
                          axis=2).reshape(KV_RANK, N_HEADS * HEAD_PAD).astype(BF16)
    wuvt = jnp.transpose(w_uv, (1, 2, 0))
    wuvt = jnp.concatenate([wuvt, jnp.zeros((N_HEADS, V_EXT - V_DIM, KV_RANK), F32)],
                           axis=1).reshape(N_HEADS * V_EXT, KV_RANK).astype(BF16)

    wukt = jnp.transpose(w_uk, (1, 2, 0))
    wukt = jnp.concatenate([jnp.zeros((N_HEADS, QK_ROPE, KV_RANK), F32), wukt,
                            jnp.zeros((N_HEADS, HEAD_PAD - QK_ROPE - QK_NOPE, KV_RANK), F32)], axis=1)
    eye = jnp.eye(N_HEADS, dtype=F32)
    wukt_bd = (eye[:, None, :, None] * wukt[:, :, None, :]).reshape(N_HEADS * HEAD_PAD, N_HEADS * KV_RANK).astype(BF16)
    wuvh = jnp.transpose(w_uv, (1, 0, 2))
    wuv_bd = (eye[:, None, :, None] * wuvh[:, :, None, :]).reshape(N_HEADS * KV_RANK, ATTN_WIDTH).astype(BF16)
    return wz, wq2, wuk, wuvt, wukt_bd, wuv_bd


def kernel(x_prompt, x_sample, c_prompt, c_sample, cache_kv_latent, cache_k_rope, state_conv, state_ffn_conv, page_table, w_ada, b_ada, norm1, w_in, q_norm, w_uq, kv_norm, w_uk, w_uv, conv_w, conv_b, conv_ln_g, conv_ln_b, out_norm_attn, out_norm_conv, w_out, norm2, w_up, ffn_conv_w, ffn_conv_b, w_down, final_norm):
    assert w_ada.shape[0] == 1, "single-layer decoder"
    n_p, s_p, _ = x_prompt.shape
    n_s, s_s, _ = x_sample.shape
    assert s_s == 1

    row = lambda a: a.reshape(1, -1)
    wz, wq2, wuk, wuvt, wukt_bd, wuv_bd = _pack_weights(w_in[0], w_uq[0], w_uk[0], w_uv[0])
    wout = w_out[0].astype(BF16)
    wup = w_up[0].astype(BF16)
    wdn = w_down[0].astype(BF16)
    norm1_r, qn_r, kvn_r = row(norm1[0]), row(q_norm[0]), row(kv_norm[0])
    cw, cb = conv_w[0], row(conv_b[0])
    lng, lnb, onc, ona = row(conv_ln_g[0]), row(conv_ln_b[0]), row(out_norm_conv[0]), row(out_norm_attn[0])
    norm2_r, fcw, fcb, fnorm = row(norm2[0]), ffn_conv_w[0], row(ffn_conv_b[0]), row(final_norm)

    n_c = n_p + n_s
    n_c_pad = -(-n_c // SUBLANES) * SUBLANES
    c_all = jnp.concatenate([c_prompt, c_sample, jnp.zeros((n_c_pad - n_c, D_MODEL), F32)], axis=0)
    mod = _modulation(c_all, w_ada[0], b_ada[0])
    mod_p = mod[:n_p].reshape(n_p, N_MOD, D_MODEL)
    mod_s = mod[n_p:n_c]

    tab_p = _rope_tables(jnp.arange(s_p))
    tabq_t = jnp.transpose(tab_p[0:2], (0, 2, 1))
    qt, k, vt, kv_lat_p, k_rope_p, cvn_p, conv_st_p = _prompt_pre(
        x_prompt, mod_p, tabq_t, tab_p[2:4], norm1_r, wz, qn_r, wq2.T, kvn_r, wuk, wuvt, cw, cb, lng, lnb, onc)
    attn_p = _flash(qt, k, vt)
    y_p, ffn_st_p = _prompt_post(x_prompt, attn_p, cvn_p, mod_p, ona, wout, norm2_r, wup, fcw, fcb, wdn, fnorm)

    tab_s = _rope_tables(PAST_LEN + jnp.arange(s_s))
    st_conv = state_conv[0].reshape(n_s, (CONV_K - 1) * CONV_WIDTH)
    qlat, q_s, lat_s, kr_s, cvn_s, st_conv_new = _sample_pre(
        x_sample[:, 0], mod_s, tab_s, norm1_r, wz, qn_r, wq2, kvn_r, wukt_bd, cw, cb, lng, lnb, onc, st_conv)
    qrope = q_s.reshape(n_s, N_HEADS, HEAD_PAD)[:, :, :QK_ROPE]
    o_lat = _decode(page_table, qlat.reshape(n_s, N_HEADS, KV_RANK), qrope,
                    lat_s.reshape(n_s, 1, KV_RANK), kr_s.reshape(n_s, 1, QK_ROPE),
                    cache_kv_latent, jnp.swapaxes(cache_k_rope, 2, 3))
    st_ffn = state_ffn_conv[0].reshape(n_s, (FFN_K - 1) * 2 * D_FF)
    y_s, st_ffn_new = _sample_post(
        x_sample[:, 0], o_lat.reshape(n_s, N_HEADS * KV_RANK), cvn_s, mod_s, wuv_bd, ona, wout, norm2_r,
        wup, fcw, fcb, wdn, fnorm, st_ffn)

    return (
        y_p,
        y_s.reshape(n_s, 1, D_MODEL),
        kv_lat_p,
        k_rope_p,
        conv_st_p,
        ffn_st_p,
        lat_s.reshape(1, n_s, 1, KV_RANK),
        kr_s.reshape(1, n_s, 1, QK_ROPE),
        st_conv_new.reshape(1, n_s, CONV_K - 1, CONV_WIDTH),
        st_ffn_new.reshape(1, n_s, FFN_K - 1, 2 * D_FF),
    )
```
